```python
import jax, jax.numpy as jnp
from jax import lax
import numpy as np

D_MODEL = 2048
BATCH = 4
SEQ = 4096
DEPTH = 1

D_MIX = D_MODEL
D_POOL = D_MIX // 2
POOL_WINDOWS = (2, 4, 8, 16)
N_POOL_GROUPS = len(POOL_WINDOWS)
POOL_GROUP = D_POOL // N_POOL_GROUPS
D_ATTN = D_MIX - D_POOL
HEAD_DIM = 64
N_Q_HEADS = D_ATTN // HEAD_DIM
N_KV_HEADS = max(1, N_Q_HEADS // 8)
GQA_GROUP = N_Q_HEADS // N_KV_HEADS
WINDOW = 128
D_KV = N_KV_HEADS * HEAD_DIM
D_IN = D_POOL + D_ATTN + 2 * D_KV
N_EXPERT_GROUPS = 8
EXPERTS_PER_GROUP = 8
N_EXPERTS = N_EXPERT_GROUPS * EXPERTS_PER_GROUP
TOP_K = 2
D_EXPERT = D_MODEL // 4
MOE_BLOCK = 128
N_MOD = 6
RMS_EPS = 1e-6
NEG_INF = -1e30

kernel_name = "hymba_pool_swa_sink_hmoe_adaln"


def rmsnorm(x, g):
    xf = x.astype(jnp.float32)
    y = xf * lax.rsqrt(jnp.mean(xf * xf, axis=-1, keepdims=True) + RMS_EPS)
    return (y * g.astype(jnp.float32)).astype(x.dtype)


def modulate(h, shift, scale):
    return h * (1 + scale[:, None, :]) + shift[:, None, :]


def pool_mixer(u, w_pool, pool_scale):
    S = u.shape[1]
    uf = u.astype(jnp.float32)
    cnt_base = jnp.arange(1, S + 1, dtype=jnp.float32)[:, None]
    outs = []
    for gi, w in enumerate(POOL_WINDOWS):
        ug = uf[..., gi * POOL_GROUP:(gi + 1) * POOL_GROUP]
        cs = jnp.cumsum(ug, axis=1)
        lag = jnp.pad(cs, ((0, 0), (w, 0), (0, 0)))[:, :S]
        mean = (cs - lag) / jnp.minimum(cnt_base, w)
        outs.append(jnp.einsum('bsc,cd->bsd', (mean - ug).astype(u.dtype), w_pool[gi]))
    return jnp.concatenate(outs, axis=-1) * pool_scale


def swa_sink_attention(q, k, v, sinks):
    B, S = q.shape[0], q.shape[1]
    nb = S // WINDOW
    qb = q.reshape(B, nb, WINDOW, N_KV_HEADS, GQA_GROUP, HEAD_DIM)
    kb = k.reshape(B, nb, WINDOW, N_KV_HEADS, HEAD_DIM)
    vb = v.reshape(B, nb, WINDOW, N_KV_HEADS, HEAD_DIM)

    def with_prev(t):
        prev = jnp.pad(t, ((0, 0), (1, 0), (0, 0), (0, 0), (0, 0)))[:, :-1]
        return jnp.concatenate([prev, t], axis=2)

    kc, vc = with_prev(kb), with_prev(vb)
    s = jnp.einsum('bnqkgd,bnskd->bnkgqs', qb, kc).astype(jnp.float32) * (HEAD_DIM ** -0.5)
    n = jnp.arange(nb)[:, None, None]
    i = jnp.arange(WINDOW)[None, :, None]
    j = jnp.arange(2 * WINDOW)[None, None, :]
    diff = i - j + WINDOW
    kpos = n * WINDOW + j - WINDOW
    mask = (diff >= 0) & (diff < WINDOW) & (kpos >= 0)
    s = jnp.where(mask[None, :, None, None], s, NEG_INF)
    sink = sinks.astype(jnp.float32).reshape(N_KV_HEADS, GQA_GROUP)[None, None, :, :, None, None]
    m = jnp.maximum(jnp.max(s, axis=-1, keepdims=True), sink)
    p = jnp.exp(s - m)
    p = p / (jnp.sum(p, axis=-1, keepdims=True) + jnp.exp(sink - m))
    o = jnp.einsum('bnkgqs,bnskd->bnqkgd', p.astype(v.dtype), vc)
    return o.reshape(B, S, N_Q_HEADS * HEAD_DIM)


def hierarchical_moe(h, w_rg, b_rg, w_re, b_re, w_gate, w_up, w_down):
    B, S, D = h.shape
    T = B * S
    xt = h.reshape(T, D)
    gl = (xt @ w_rg).astype(jnp.float32) + b_rg.astype(jnp.float32)
    gp = jax.nn.softmax(gl, axis=-1)
    g_sel = jnp.argmax(gl, axis=-1)
    p_grp = jnp.take_along_axis(gp, g_sel[:, None], axis=-1)
    el = ((xt @ w_re).astype(jnp.float32) + b_re.astype(jnp.float32)).reshape(T, N_EXPERT_GROUPS, EXPERTS_PER_GROUP)
    el_sel = jnp.take_along_axis(el, g_sel[:, None, None], axis=1)[:, 0]
    top_l, top_i = lax.top_k(el_sel, TOP_K)
    comb = p_grp * jax.nn.softmax(top_l, axis=-1)
    eid = (g_sel[:, None] * EXPERTS_PER_GROUP + top_i).reshape(-1)
    tok = jnp.repeat(jnp.arange(T, dtype=jnp.int32), TOP_K)
    wts = comb.reshape(-1)
    A = T * TOP_K
    order = jnp.argsort(eid)
    e_s, tok_s, w_s = eid[order], tok[order], wts[order]
    counts = jnp.zeros((N_EXPERTS,), jnp.int32).at[eid].add(1)
    pcounts = (counts + MOE_BLOCK - 1) // MOE_BLOCK * MOE_BLOCK
    starts = jnp.cumsum(counts) - counts
    pends = jnp.cumsum(pcounts)
    pstarts = pends - pcounts
    dest = pstarts[e_s] + jnp.arange(A, dtype=jnp.int32) - starts[e_s]
    P = A + N_EXPERTS * MOE_BLOCK
    nblk = P // MOE_BLOCK
    row_tok = jnp.full((P,), T, jnp.int32).at[dest].set(tok_s)
    row_w = jnp.zeros((P,), jnp.float32).at[dest].set(w_s)
    blk_e = jnp.minimum(jnp.searchsorted(pends, jnp.arange(nblk) * MOE_BLOCK, side='right'), N_EXPERTS - 1)
    xpad = jnp.concatenate([xt, jnp.zeros((1, D), xt.dtype)], axis=0)
    xr = xpad[row_tok].reshape(nblk, MOE_BLOCK, D)

    def expert_block(args):
        xb, e = args
        return (jax.nn.silu(xb @ w_gate[e]) * (xb @ w_up[e])) @ w_down[e]

    yr = lax.map(expert_block, (xr, blk_e)).reshape(P, D)
    yr = yr * row_w[:, None].astype(yr.dtype)
    out = jnp.zeros((T + 1, D), yr.dtype).at[row_tok].add(yr)[:T]
    return out.reshape(B, S, D)


def setup_inputs(seed: int = 0) -> dict:
    key = jax.random.key(seed)
    ks = jax.random.split(key, 20)
    f32 = jnp.float32
    nrm = lambda k, shp, s: jax.random.normal(k, shp, f32) * s
    L = DEPTH
    return {
        "x": nrm(ks[0], (BATCH, SEQ, D_MODEL), 1.0),
        "c": nrm(ks[1], (BATCH, D_MODEL), 1.0),
        "w_ada": nrm(ks[2], (L, D_MODEL, N_MOD * D_MODEL), 0.5 * D_MODEL ** -0.5),
        "b_ada": nrm(ks[3], (L, N_MOD * D_MODEL), 0.02),
        "g_norm1": 1.0 + nrm(ks[4], (L, D_MODEL), 0.02),
        "w_in": nrm(ks[5], (L, D_MODEL, D_IN), D_MODEL ** -0.5),
        "w_pool": nrm(ks[6], (L, N_POOL_GROUPS, POOL_GROUP, POOL_GROUP), POOL_GROUP ** -0.5),
        "pool_scale": 1.0 + nrm(ks[7], (L, D_POOL), 0.1),
        "attn_sinks": nrm(ks[8], (L, N_Q_HEADS), 0.5),
        "w_out": nrm(ks[9], (L, D_MIX, D_MODEL), D_MIX ** -0.5),
        "g_norm2": 1.0 + nrm(ks[10], (L, D_MODEL), 0.02),
        "w_router_group": nrm(ks[11], (L, D_MODEL, N_EXPERT_GROUPS), D_MODEL ** -0.5),
        "b_router_group": nrm(ks[12], (L, N_EXPERT_GROUPS), 0.01),
        "w_router_expert": nrm(ks[13], (L, D_MODEL, N_EXPERTS), D_MODEL ** -0.5),
        "b_router_expert": nrm(ks[14], (L, N_EXPERTS), 0.01),
        "w_gate": nrm(ks[15], (L, N_EXPERTS, D_MODEL, D_EXPERT), D_MODEL ** -0.5),
        "w_up": nrm(ks[16], (L, N_EXPERTS, D_MODEL, D_EXPERT), D_MODEL ** -0.5),
        "w_down": nrm(ks[17], (L, N_EXPERTS, D_EXPERT, D_MODEL), D_EXPERT ** -0.5),
        "g_final": 1.0 + nrm(ks[18], (D_MODEL,), 0.02),
    }


def reference(x, c, w_ada, b_ada, g_norm1, w_in, w_pool, pool_scale, attn_sinks, w_out,
              g_norm2, w_router_group, b_router_group, w_router_expert, b_router_expert,
              w_gate, w_up, w_down, g_final):
    B, S, _ = x.shape
    c_act = jax.nn.silu(c)
    for l in range(DEPTH):
        mod = c_act @ w_ada[l] + b_ada[l]
        sh1, sc1, gt1, sh2, sc2, gt2 = jnp.split(mod, N_MOD, axis=-1)
        h = modulate(rmsnorm(x, g_norm1[l]), sh1, sc1)
        proj = h @ w_in[l]
        u_pool = proj[..., :D_POOL]
        q = proj[..., D_POOL:D_POOL + D_ATTN].reshape(B, S, N_Q_HEADS, HEAD_DIM)
        k = proj[..., D_POOL + D_ATTN:D_POOL + D_ATTN + D_KV].reshape(B, S, N_KV_HEADS, HEAD_DIM)
        v = proj[..., D_POOL + D_ATTN + D_KV:].reshape(B, S, N_KV_HEADS, HEAD_DIM)
        y_pool = pool_mixer(u_pool, w_pool[l], pool_scale[l])
        y_attn = swa_sink_attention(q, k, v, attn_sinks[l])
        mix = jnp.concatenate([y_pool, y_attn], axis=-1)
        x = x + gt1[:, None, :] * (mix @ w_out[l])
        h = modulate(rmsnorm(x, g_norm2[l]), sh2, sc2)
        y = hierarchical_moe(h, w_router_group[l], b_router_group[l], w_router_expert[l],
                             b_router_expert[l], w_gate[l], w_up[l], w_down[l])
        x = x + gt2[:, None, :] * y
    return rmsnorm(x, g_final)
```

```python
import functools

import jax
import jax.numpy as jnp
from jax import lax
from jax.experimental import pallas as pl
from jax.experimental.pallas import tpu as pltpu

F32 = jnp.float32
BF16 = jnp.bfloat16
I32 = jnp.int32

D_POOL = 1024
POOL_WINDOWS = (2, 4, 8, 16)
POOL_GROUP = 256
MAX_POOL_WINDOW = 16
HEAD_DIM = 64
N_KV_HEADS = 2
GQA_GROUP = 8
WINDOW = 128
D_ATTN = 1024
D_KV = 128
N_EXPERT_GROUPS = 8
EXPERTS_PER_GROUP = 8
N_EXPERTS = 64
D_EXPERT = 512
N_MOD = 6
RMS_EPS = 1e-6
NEG_INF = -1e30

LANES = 128
HEADS_PER_VREG = LANES // HEAD_DIM
PAIRS_PER_KV = GQA_GROUP // HEADS_PER_VREG

ADA_TN = 1024
INPROJ_TM = 512
MIX_TQ = 256
MOE_BLK = 256
PAD_PIECES = tuple(MOE_BLK >> (k + 1) for k in range(MOE_BLK.bit_length() - 1))
DISPATCH_TC = 2048
COMBINE_TK = 256
DMA_UNROLL = 8
ROUTER_LANES = 128
VMEM_LIMIT = 56 * 1024 * 1024


def _silu(v):
    return v * jax.nn.sigmoid(v)


def _ada_kernel(c_ref, w_ref, b_ref, o_ref):
    ca = _silu(c_ref[...])
    o_ref[...] = jnp.dot(ca.astype(BF16), w_ref[...].astype(BF16),
                         preferred_element_type=F32) + b_ref[...]


def _ada(c, w, b):
    bsz, d = c.shape
    n = w.shape[1]
    return pl.pallas_call(
        _ada_kernel,
        out_shape=jax.ShapeDtypeStruct((bsz, n), F32),
        grid=(n // ADA_TN,),
        in_specs=[
            pl.BlockSpec((bsz, d), lambda i: (0, 0)),
            pl.BlockSpec((d, ADA_TN), lambda i: (0, i)),
            pl.BlockSpec((1, ADA_TN), lambda i: (0, i)),
        ],
        out_specs=pl.BlockSpec((bsz, ADA_TN), lambda i: (0, i)),
        compiler_params=pltpu.CompilerParams(
            dimension_semantics=("arbitrary",), vmem_limit_bytes=VMEM_LIMIT),
        name="ada",
    )(c, w, b.reshape(1, n))


def _norm_mod(x, g, shift, scale):
    ms = jnp.mean(x * x, axis=-1, keepdims=True)
    return (x * lax.rsqrt(ms + RMS_EPS)) * g * (1.0 + scale) + shift


def _inproj_kernel(x_ref, mod_ref, g_ref, w_ref, o_ref):
    h = _norm_mod(x_ref[...], g_ref[...], mod_ref[0, 0:1, :], mod_ref[0, 1:2, :])
    o_ref[...] = jnp.dot(h.astype(BF16), w_ref[...],
                         preferred_element_type=F32).astype(o_ref.dtype)


def _inproj(x2, mod3, g1, w_in, seq):
    t, d = x2.shape
    n = w_in.shape[1]
    per_b = seq // INPROJ_TM
    return pl.pallas_call(
        _inproj_kernel,
        out_shape=jax.ShapeDtypeStruct((t, n), BF16),
        grid=(t // INPROJ_TM,),
        in_specs=[
            pl.BlockSpec((INPROJ_TM, d), lambda i: (i, 0)),
            pl.BlockSpec((1, N_MOD, d), lambda i: (i // per_b, 0, 0)),
            pl.BlockSpec((1, d), lambda i: (0, 0)),
            pl.BlockSpec((d, n), lambda i: (0, 0)),
        ],
        out_specs=pl.BlockSpec((INPROJ_TM, n), lambda i: (i, 0)),
        compiler_params=pltpu.CompilerParams(
            dimension_semantics=("arbitrary",), vmem_limit_bytes=VMEM_LIMIT),
        name="inproj",
    )(x2, mod3, g1, w_in)


def _mix_kernel(sinks_ref, proj_ref, kvp_ref, up_ref, x_ref, mod_ref, wpool_ref,
                pscale_ref, wout_ref, g2_ref, wr_ref, br_ref,
                x1_ref, h2_ref, mi_ref, mf_ref, cnt_ref,
                ubuf, mixbuf, carry):
    b = pl.program_id(0)
    j = pl.program_id(1)
    tq = x_ref.shape[0]
    seq_start = j == 0

    @pl.when(jnp.logical_and(b == 0, j == 0))
    def _():
        carry[...] = jnp.zeros_like(carry)

    halo = up_ref[...].astype(F32)
    ubuf[0:MAX_POOL_WINDOW, :] = jnp.where(seq_start, 0.0, halo)
    ubuf[MAX_POOL_WINDOW:MAX_POOL_WINDOW + tq, :] = proj_ref[:, 0:D_POOL].astype(F32)
    pos = j * tq + lax.broadcasted_iota(I32, (tq, 1), 0)
    for gi, w in enumerate(POOL_WINDOWS):
        c0 = gi * POOL_GROUP
        u = ubuf[MAX_POOL_WINDOW:MAX_POOL_WINDOW + tq, c0:c0 + POOL_GROUP]
        acc = u
        for k in range(1, w):
            acc = acc + ubuf[MAX_POOL_WINDOW - k:MAX_POOL_WINDOW - k + tq, c0:c0 + POOL_GROUP]
        cnt = jnp.minimum(pos + 1, w).astype(F32)
        delta = (acc / cnt - u).astype(BF16)
        yp = jnp.dot(delta, wpool_ref[gi], preferred_element_type=F32)
        yp = yp * pscale_ref[:, c0:c0 + POOL_GROUP]
        mixbuf[:, c0:c0 + POOL_GROUP] = yp.astype(BF16)

    lane = lax.broadcasted_iota(I32, (2 * WINDOW, LANES), 1)
    low = lane < HEAD_DIM
    qi = lax.broadcasted_iota(I32, (WINDOW, 2 * WINDOW), 0)
    kj = lax.broadcasted_iota(I32, (WINDOW, 2 * WINDOW), 1)
    dist = qi - kj + WINDOW
    band = jnp.logical_and(dist >= 0, dist < WINDOW)
    olane = lax.broadcasted_iota(I32, (WINDOW, LANES), 1)
    k_col = D_POOL + D_ATTN
    v_col = k_col + D_KV
    for blk in range(tq // WINDOW):
        r0 = blk * WINDOW
        if blk == 0:
            kcat = jnp.concatenate(
                [kvp_ref[:, 0:D_KV], proj_ref[0:WINDOW, k_col:k_col + D_KV]], axis=0)
            vcat = jnp.concatenate(
                [kvp_ref[:, D_KV:2 * D_KV], proj_ref[0:WINDOW, v_col:v_col + D_KV]], axis=0)
            mask = jnp.logical_and(band, kj >= jnp.where(seq_start, WINDOW, 0))
        else:
            kcat = proj_ref[r0 - WINDOW:r0 + WINDOW, k_col:k_col + D_KV]
            vcat = proj_ref[r0 - WINDOW:r0 + WINDOW, v_col:v_col + D_KV]
            mask = band
        kswap = pltpu.roll(kcat.astype(F32), HEAD_DIM, 1).astype(BF16)
        vswap = pltpu.roll(vcat.astype(F32), HEAD_DIM, 1).astype(BF16)
        zero = jnp.zeros_like(kcat)
        for g in range(N_KV_HEADS):
            ksrc_lo, ksrc_hi = (kcat, kswap) if g == 0 else (kswap, kcat)
            vsrc_lo, vsrc_hi = (vcat, vswap) if g == 0 else (vswap, vcat)
            kbd = jnp.concatenate([jnp.where(low, ksrc_lo, zero),
                                   jnp.where(low, zero, ksrc_hi)], axis=0)
            vbd = jnp.concatenate([jnp.where(low, vsrc_lo, zero),
                                   jnp.where(low, zero, vsrc_hi)], axis=0)
            q_col = D_POOL + g * GQA_GROUP * HEAD_DIM
            q = jnp.concatenate(
                [proj_ref[r0:r0 + WINDOW, q_col + p * LANES:q_col + (p + 1) * LANES]
                 for p in range(PAIRS_PER_KV)], axis=0)
            q = q * jnp.asarray(HEAD_DIM ** -0.5, BF16)
            s = lax.dot_general(q, kbd, (((1,), (1,)), ((), ())),
                                preferred_element_type=F32)
            probs = []
            rdens = []
            for p in range(PAIRS_PER_KV):
                row_p = []
                row_r = []
                for hh in range(HEADS_PER_VREG):
                    sink = sinks_ref[g * GQA_GROUP + p * HEADS_PER_VREG + hh]
                    sp = s[p * WINDOW:(p + 1) * WINDOW,
                           hh * 2 * WINDOW:(hh + 1) * 2 * WINDOW]
                    sp = jnp.where(mask, sp, NEG_INF)
                    m = jnp.maximum(jnp.max(sp, axis=-1, keepdims=True), sink)
                    e = jnp.exp(sp - m)
                    den = jnp.sum(e, axis=-1, keepdims=True) + jnp.exp(sink - m)
                    row_p.append(e.astype(BF16))
                    row_r.append(1.0 / den)
                probs.append(jnp.concatenate(row_p, axis=1))
                rdens.append(row_r)
            pmat = jnp.concatenate(probs, axis=0)
            o = jnp.dot(pmat, vbd, preferred_element_type=F32)
            for p in range(PAIRS_PER_KV):
                op = o[p * WINDOW:(p + 1) * WINDOW, :]
                norm = jnp.where(olane < HEAD_DIM, rdens[p][0], rdens[p][1])
                c0 = D_POOL + g * GQA_GROUP * HEAD_DIM + p * LANES
                mixbuf[r0:r0 + WINDOW, c0:c0 + LANES] = (op * norm).astype(BF16)

    y = jnp.dot(mixbuf[...], wout_ref[...], preferred_element_type=F32)
    x1 = x_ref[...] + mod_ref[0, 2:3, :] * y
    x1_ref[...] = x1

    h2 = _norm_mod(x1, g2_ref[...], mod_ref[0, 3:4, :], mod_ref[0, 4:5, :])
    for sidx in range(h2_ref.shape[1]):
        h2_ref[:, sidx, :] = h2[:, sidx * LANES:(sidx + 1) * LANES]

    logits = jnp.dot(h2.astype(BF16), wr_ref[...], preferred_element_type=F32) + br_ref[...]
    ln = lax.broadcasted_iota(I32, (tq, ROUTER_LANES), 1)
    lnf = ln.astype(F32)
    ninf = -jnp.inf
    is_g = ln < N_EXPERT_GROUPS
    gl = jnp.where(is_g, logits, ninf)
    gmax = jnp.max(gl, axis=-1, keepdims=True)
    g_sel = jnp.min(jnp.where(gl == gmax, lnf, float(ROUTER_LANES)), axis=-1, keepdims=True)
    p_grp = 1.0 / jnp.sum(jnp.where(is_g, jnp.exp(logits - gmax), 0.0), axis=-1, keepdims=True)
    lane_grp = ((ln - N_EXPERT_GROUPS) >> 3).astype(F32)
    in_sel = jnp.logical_and(
        jnp.logical_and(ln >= N_EXPERT_GROUPS, ln < N_EXPERT_GROUPS + N_EXPERTS),
        lane_grp == g_sel)
    el = jnp.where(in_sel, logits, ninf)
    l1 = jnp.max(el, axis=-1, keepdims=True)
    i1 = jnp.min(jnp.where(el == l1, lnf, float(ROUTER_LANES)), axis=-1, keepdims=True)
    el2 = jnp.where(lnf == i1, ninf, el)
    l2 = jnp.max(el2, axis=-1, keepdims=True)
    i2 = jnp.min(jnp.where(el2 == l2, lnf, float(ROUTER_LANES)), axis=-1, keepdims=True)
    tt = jnp.exp(l2 - l1)
    w1 = 1.0 / (1.0 + tt)
    comb1 = p_grp * w1
    comb2 = p_grp * (tt * w1)
    hit1 = lnf == i1
    hit2 = lnf == i2
    onehot = jnp.where(jnp.logical_or(hit1, hit2), 1.0, 0.0)
    rr = lax.broadcasted_iota(I32, (tq, tq), 0)
    cc = lax.broadcasted_iota(I32, (tq, tq), 1)
    lower = jnp.where(rr > cc, 1.0, 0.0).astype(BF16)
    prior = jnp.dot(lower, onehot.astype(BF16), preferred_element_type=F32) + carry[...]
    rank1 = jnp.sum(jnp.where(hit1, prior, 0.0), axis=-1, keepdims=True)
    rank2 = jnp.sum(jnp.where(hit2, prior, 0.0), axis=-1, keepdims=True)
    carry[...] = carry[...] + jnp.sum(onehot, axis=0, keepdims=True)
    cnt_ref[...] = carry[...].astype(I32)
    eid1 = i1 - float(N_EXPERT_GROUPS)
    eid2 = i2 - float(N_EXPERT_GROUPS)
    meta = jnp.where(ln == 0, eid1, jnp.where(ln == 1, eid2,
                     jnp.where(ln == 2, rank1, jnp.where(ln == 3, rank2, 0.0))))
    mi_ref[...] = meta.astype(I32)
    mf_ref[...] = jnp.where(ln == 0, comb1, jnp.where(ln == 1, comb2, 0.0))


def _mix(sinks, proj, x2, mod3, w_pool, pool_scale, w_out, g2, w_r, b_r, bsz, seq):
    t, d = x2.shape
    n_in = proj.shape[1]
    per_b = seq // MIX_TQ
    q_per_win = MIX_TQ // WINDOW
    q_per_halo = MIX_TQ // MAX_POOL_WINDOW
    slabs = d // LANES

    def row(b, j):
        return b * per_b + j

    out_shapes = (
        jax.ShapeDtypeStruct((t, d), F32),
        jax.ShapeDtypeStruct((t, slabs, LANES), F32),
        jax.ShapeDtypeStruct((t, ROUTER_LANES), I32),
        jax.ShapeDtypeStruct((t, ROUTER_LANES), F32),
        jax.ShapeDtypeStruct((1, ROUTER_LANES), I32),
    )
    return pl.pallas_call(
        _mix_kernel,
        out_shape=out_shapes,
        grid=(bsz, per_b),
        in_specs=[
            pl.BlockSpec(memory_space=pltpu.SMEM),
            pl.BlockSpec((MIX_TQ, n_in), lambda b, j: (row(b, j), 0)),
            pl.BlockSpec((WINDOW, 2 * D_KV),
                         lambda b, j: (jnp.maximum(row(b, j) * q_per_win - 1, 0),
                                       (D_POOL + D_ATTN) // (2 * D_KV))),
            pl.BlockSpec((MAX_POOL_WINDOW, D_POOL),
                         lambda b, j: (jnp.maximum(row(b, j) * q_per_halo - 1, 0), 0)),
            pl.BlockSpec((MIX_TQ, d), lambda b, j: (row(b, j), 0)),
            pl.BlockSpec((1, N_MOD, d), lambda b, j: (b, 0, 0)),
            pl.BlockSpec(w_pool.shape, lambda b, j: (0, 0, 0)),
            pl.BlockSpec((1, D_POOL), lambda b, j: (0, 0)),
            pl.BlockSpec((d, d), lambda b, j: (0, 0)),
            pl.BlockSpec((1, d), lambda b, j: (0, 0)),
            pl.BlockSpec((d, ROUTER_LANES), lambda b, j: (0, 0)),
            pl.BlockSpec((1, ROUTER_LANES), lambda b, j: (0, 0)),
        ],
        out_specs=(
            pl.BlockSpec((MIX_TQ, d), lambda b, j: (row(b, j), 0)),
            pl.BlockSpec((MIX_TQ, slabs, LANES), lambda b, j: (row(b, j), 0, 0)),
            pl.BlockSpec((MIX_TQ, ROUTER_LANES), lambda b, j: (row(b, j), 0)),
            pl.BlockSpec((MIX_TQ, ROUTER_LANES), lambda b, j: (row(b, j), 0)),
            pl.BlockSpec((1, ROUTER_LANES), lambda b, j: (0, 0)),
        ),
        scratch_shapes=[
            pltpu.VMEM((MAX_POOL_WINDOW + MIX_TQ, D_POOL), F32),
            pltpu.VMEM((MIX_TQ, d), BF16),
            pltpu.VMEM((1, ROUTER_LANES), F32),
        ],
        compiler_params=pltpu.CompilerParams(
            dimension_semantics=("arbitrary", "arbitrary"), vmem_limit_bytes=VMEM_LIMIT),
        name="mix",
    )(sinks, proj, proj, proj, x2, mod3, w_pool, pool_scale, w_out, g2, w_r, b_r)


def _dispatch_kernel(pstart_ref, count_ref, e1_ref, e2_ref, r1_ref, r2_ref, h2_hbm,
                     xr_hbm, d1_ref, d2_ref, zbuf, sem, zsem):
    tc = e1_ref.shape[0]
    step = pl.program_id(0)
    base = step * tc
    nblk = xr_hbm.shape[0] // MOE_BLK
    last = N_EXPERTS - 1
    n_used = (pstart_ref[last] + count_ref[last] + MOE_BLK - 1) // MOE_BLK

    def pad_copies(e):
        cnt = count_ref[e]
        npad = (-cnt) & (MOE_BLK - 1)
        off = pstart_ref[e] + cnt
        out = []
        for piece in PAD_PIECES:
            out.append((npad & piece, pltpu.make_async_copy(
                zbuf.at[pl.ds(0, piece)], xr_hbm.at[pl.ds(off, piece)], zsem)))
            off = off + (npad & piece)
        return out

    def tail_copy(blk):
        return pltpu.make_async_copy(zbuf, xr_hbm.at[pl.ds(blk * MOE_BLK, MOE_BLK)], zsem)

    def zero_fill(start):
        def per_expert(e, carry):
            for flag, cp in pad_copies(e):
                pl.when(flag != 0)(cp.start if start else cp.wait)
            return carry

        def per_tail(blk, carry):
            cp = tail_copy(blk)
            cp.start() if start else cp.wait()
            return carry

        lax.fori_loop(0, N_EXPERTS, per_expert, 0)
        lax.fori_loop(n_used, nblk, per_tail, 0)

    @pl.when(step == 0)
    def _():
        zbuf[...] = jnp.zeros_like(zbuf)
        zero_fill(True)

    def copies(tl):
        src = h2_hbm.at[base + tl]
        return (pltpu.make_async_copy(src, xr_hbm.at[d1_ref[tl]], sem),
                pltpu.make_async_copy(src, xr_hbm.at[d2_ref[tl]], sem))

    def issue(i, carry):
        for u in range(DMA_UNROLL):
            tl = i * DMA_UNROLL + u
            d1_ref[tl] = pstart_ref[e1_ref[tl]] + r1_ref[tl]
            d2_ref[tl] = pstart_ref[e2_ref[tl]] + r2_ref[tl]
            for cp in copies(tl):
                cp.start()
        return carry

    def drain(i, carry):
        for u in range(DMA_UNROLL):
            for cp in copies(i * DMA_UNROLL + u):
                cp.wait()
        return carry

    lax.fori_loop(0, tc // DMA_UNROLL, issue, 0)
    lax.fori_loop(0, tc // DMA_UNROLL, drain, 0)

    @pl.when(step == 0)
    def _():
        zero_fill(False)


def _dispatch(pstarts, counts, e1, e2, r1, r2, h2r, n_rows):
    t = e1.shape[0]
    tc = min(DISPATCH_TC, t)
    smem_blk = pl.BlockSpec((tc,), lambda i: (i,), memory_space=pltpu.SMEM)
    return pl.pallas_call(
        _dispatch_kernel,
        out_shape=(
            jax.ShapeDtypeStruct((n_rows,) + h2r.shape[1:], h2r.dtype),
            jax.ShapeDtypeStruct((t,), I32),
            jax.ShapeDtypeStruct((t,), I32),
        ),
        grid=(t // tc,),
        in_specs=[
            pl.BlockSpec(memory_space=pltpu.SMEM),
            pl.BlockSpec(memory_space=pltpu.SMEM),
            smem_blk, smem_blk, smem_blk, smem_blk,
            pl.BlockSpec(memory_space=pl.ANY),
        ],
        out_specs=(pl.BlockSpec(memory_space=pl.ANY), smem_blk, smem_blk),
        scratch_shapes=[pltpu.VMEM((MOE_BLK,) + h2r.shape[1:], h2r.dtype),
                        pltpu.SemaphoreType.DMA(()), pltpu.SemaphoreType.DMA(())],
        compiler_params=pltpu.CompilerParams(
            dimension_semantics=("arbitrary",), has_side_effects=True),
        name="dispatch",
    )(pstarts, counts, e1, e2, r1, r2, h2r)


def _experts_kernel(blk_e_ref, blk_row_ref, blk_valid_ref, blk_first_ref,
                    xr_ref, wg_ref, wu_ref, wd_ref, yr_ref,
                    wg_bf, wu_bf, wd_bf, xb):
    i = pl.program_id(0)
    nvalid = blk_valid_ref[i]

    @pl.when(blk_first_ref[i] == 1)
    def _():
        wg_bf[...] = wg_ref[0].astype(BF16)
        wu_bf[...] = wu_ref[0].astype(BF16)
        wd_bf[...] = wd_ref[0].astype(BF16)

    @pl.when(nvalid > 0)
    def _():
        rows = xr_ref.shape[0]
        live = lax.broadcasted_iota(I32, (rows, 1), 0) < nvalid
        for sidx in range(xr_ref.shape[1]):
            xb[:, sidx * LANES:(sidx + 1) * LANES] = jnp.where(
                live, xr_ref[:, sidx, :], 0.0).astype(BF16)
        xv = xb[...]
        gate = jnp.dot(xv, wg_bf[...], preferred_element_type=F32)
        up = jnp.dot(xv, wu_bf[...], preferred_element_type=F32)
        act = (_silu(gate) * up).astype(BF16)
        y = jnp.dot(act, wd_bf[...], preferred_element_type=F32)
        for sidx in range(yr_ref.shape[1]):
            yr_ref[:, sidx, :] = y[:, sidx * LANES:(sidx + 1) * LANES]

    @pl.when(nvalid == 0)
    def _():
        yr_ref[...] = jnp.zeros_like(yr_ref)


def _experts(blk_e, blk_row, blk_valid, blk_first, xr, w_gate, w_up, w_down):
    n_rows, slabs, _ = xr.shape
    d = slabs * LANES
    nblk = n_rows // MOE_BLK
    grid_spec = pltpu.PrefetchScalarGridSpec(
        num_scalar_prefetch=4,
        grid=(nblk,),
        in_specs=[
            pl.BlockSpec((MOE_BLK, slabs, LANES), lambda i, be, br, bv, bf: (br[i], 0, 0)),
            pl.BlockSpec((1, d, D_EXPERT), lambda i, be, br, bv, bf: (be[i], 0, 0)),
            pl.BlockSpec((1, d, D_EXPERT), lambda i, be, br, bv, bf: (be[i], 0, 0)),
            pl.BlockSpec((1, D_EXPERT, d), lambda i, be, br, bv, bf: (be[i], 0, 0)),
        ],
        out_specs=pl.BlockSpec((MOE_BLK, slabs, LANES), lambda i, be, br, bv, bf: (i, 0, 0)),
        scratch_shapes=[
            pltpu.VMEM((d, D_EXPERT), BF16),
            pltpu.VMEM((d, D_EXPERT), BF16),
            pltpu.VMEM((D_EXPERT, d), BF16),
            pltpu.VMEM((MOE_BLK, d), BF16),
        ],
    )
    return pl.pallas_call(
        _experts_kernel,
        out_shape=jax.ShapeDtypeStruct(xr.shape, F32),
        grid_spec=grid_spec,
        compiler_params=pltpu.CompilerParams(
            dimension_semantics=("arbitrary",), vmem_limit_bytes=VMEM_LIMIT),
        name="experts",
    )(blk_e, blk_row, blk_valid, blk_first, xr, w_gate, w_up, w_down)


def _combine_kernel(d1_ref, d2_ref, yr_hbm, x1_ref, mf_ref, mod_ref, gf_ref, o_ref,
                    ya0, yb0, ya1, yb1, sems):
    i = pl.program_id(0)
    n = pl.num_programs(0)
    tk = x1_ref.shape[0]
    bufs = ((ya0, yb0), (ya1, yb1))

    def copies(tile, slot, tl):
        tok = tile * tk + tl
        ya, yb = bufs[slot]
        return (pltpu.make_async_copy(yr_hbm.at[d1_ref[tok]], ya.at[tl], sems.at[slot]),
                pltpu.make_async_copy(yr_hbm.at[d2_ref[tok]], yb.at[tl], sems.at[slot]))

    def issue(tile, slot):
        def body(k, carry):
            for u in range(DMA_UNROLL):
                for cp in copies(tile, slot, k * DMA_UNROLL + u):
                    cp.start()
            return carry
        lax.fori_loop(0, tk // DMA_UNROLL, body, 0)

    def drain(tile, slot):
        def body(k, carry):
            for u in range(DMA_UNROLL):
                for cp in copies(tile, slot, k * DMA_UNROLL + u):
                    cp.wait()
            return carry
        lax.fori_loop(0, tk // DMA_UNROLL, body, 0)

    def step(slot):
        @pl.when(i == 0)
        def _():
            issue(i, slot)

        @pl.when(i + 1 < n)
        def _():
            issue(i + 1, 1 - slot)

        drain(i, slot)
        ya, yb = bufs[slot]
        c1 = mf_ref[:, 0:1]
        c2 = mf_ref[:, 1:2]
        moe = jnp.concatenate(
            [c1 * ya[:, sidx, :] + c2 * yb[:, sidx, :] for sidx in range(ya.shape[1])], axis=1)
        xo = x1_ref[...] + mod_ref[0, 5:6, :] * moe
        ms = jnp.mean(xo * xo, axis=-1, keepdims=True)
        o_ref[...] = (xo * lax.rsqrt(ms + RMS_EPS)) * gf_ref[...]

    @pl.when(i % 2 == 0)
    def _():
        step(0)

    @pl.when(i % 2 == 1)
    def _():
        step(1)


def _combine(d1, d2, yr, x1, mf, mod3, g_final, seq):
    t, d = x1.shape
    slabs = yr.shape[1]
    per_b = seq // COMBINE_TK
    grid_spec = pltpu.PrefetchScalarGridSpec(
        num_scalar_prefetch=2,
        grid=(t // COMBINE_TK,),
        in_specs=[
            pl.BlockSpec(memory_space=pl.ANY),
            pl.BlockSpec((COMBINE_TK, d), lambda i, a, b: (i, 0)),
            pl.BlockSpec((COMBINE_TK, ROUTER_LANES), lambda i, a, b: (i, 0)),
            pl.BlockSpec((1, N_MOD, d), lambda i, a, b: (i // per_b, 0, 0)),
            pl.BlockSpec((1, d), lambda i, a, b: (0, 0)),
        ],
        out_specs=pl.BlockSpec((COMBINE_TK, d), lambda i, a, b: (i, 0)),
        scratch_shapes=[pltpu.VMEM((COMBINE_TK, slabs, LANES), F32) for _ in range(4)]
        + [pltpu.SemaphoreType.DMA((2,))],
    )
    return pl.pallas_call(
        _combine_kernel,
        out_shape=jax.ShapeDtypeStruct((t, d), F32),
        grid_spec=grid_spec,
        compiler_params=pltpu.CompilerParams(
            dimension_semantics=("arbitrary",), vmem_limit_bytes=VMEM_LIMIT),
        name="combine",
    )(d1, d2, yr, x1, mf, mod3, g_final)


def _block_schedule(counts, n_rows):
    nblk = n_rows // MOE_BLK
    pcounts = (counts + MOE_BLK - 1) // MOE_BLK * MOE_BLK
    pends = jnp.cumsum(pcounts)
    pstarts = pends - pcounts
    blk0 = jnp.arange(nblk, dtype=I32) * MOE_BLK
    n_used = pends[-1] // MOE_BLK
    used = jnp.arange(nblk, dtype=I32) < n_used
    e_of = jnp.minimum(jnp.searchsorted(pends, blk0, side="right"), N_EXPERTS - 1).astype(I32)
    last = jnp.maximum(n_used - 1, 0)
    blk_row = jnp.where(used, jnp.arange(nblk, dtype=I32), last).astype(I32)
    blk_e = jnp.where(used, e_of, e_of[last]).astype(I32)
    valid = jnp.clip(counts[e_of] - (blk0 - pstarts[e_of]), 0, MOE_BLK)
    blk_valid = jnp.where(used, valid, 0).astype(I32)
    prev_e = jnp.concatenate([jnp.full((1,), -1, I32), blk_e[:-1]])
    blk_first = jnp.logical_and(used, blk_e != prev_e).astype(I32)
    return pstarts.astype(I32), blk_e, blk_row, blk_valid, blk_first


def kernel(x, c, w_ada, b_ada, g_norm1, w_in, w_pool, pool_scale, attn_sinks, w_out,
           g_norm2, w_router_group, b_router_group, w_router_expert, b_router_expert,
           w_gate, w_up, w_down, g_final):
    bsz, seq, d = x.shape
    t = bsz * seq
    assert w_ada.shape[0] == 1, "single-layer model: the combine step applies the final norm"
    x2 = x.reshape(t, d)
    for l in range(1):
        mod3 = _ada(c, w_ada[l], b_ada[l]).reshape(bsz, N_MOD, d)
        proj = _inproj(x2, mod3, g_norm1[l].reshape(1, d), w_in[l].astype(BF16), seq)
        pad = ROUTER_LANES - N_EXPERT_GROUPS - N_EXPERTS
        w_r = jnp.concatenate(
            [w_router_group[l], w_router_expert[l], jnp.zeros((d, pad), F32)], axis=1).astype(BF16)
        b_r = jnp.concatenate(
            [b_router_group[l], b_router_expert[l], jnp.zeros((pad,), F32)]).reshape(1, ROUTER_LANES)
        x1, h2r, mi, mf, cnt = _mix(
            attn_sinks[l], proj, x2, mod3, w_pool[l].astype(BF16),
            pool_scale[l].reshape(1, D_POOL), w_out[l].astype(BF16),
            g_norm2[l].reshape(1, d), w_r, b_r, bsz, seq)
        n_rows = 2 * t + N_EXPERTS * MOE_BLK
        counts = cnt[0, N_EXPERT_GROUPS:N_EXPERT_GROUPS + N_EXPERTS]
        pstarts, blk_e, blk_row, blk_valid, blk_first = _block_schedule(counts, n_rows)
        xr, d1, d2 = _dispatch(pstarts, counts, mi[:, 0], mi[:, 1], mi[:, 2], mi[:, 3], h2r, n_rows)
        yr = _experts(blk_e, blk_row, blk_valid, blk_first, xr, w_gate[l], w_up[l], w_down[l])
        x2 = _combine(d1, d2, yr, x1, mf, mod3, g_final.reshape(1, d), seq)
    return x2.reshape(bsz, seq, d)
```

```python
import functools

import jax
import jax.numpy as jnp
from jax import lax
from jax.experimental import pallas as pl
from jax.experimental.pallas import tpu as pltpu

F32 = jnp.float32
BF16 = jnp.bfloat16
I32 = jnp.int32
U32 = jnp.uint32

D_POOL = 1024
POOL_WINDOWS = (2, 4, 8, 16)
POOL_GROUP = 256
MAX_POOL_WINDOW = 16
HEAD_DIM = 64
N_KV_HEADS = 2
GQA_GROUP = 8
WINDOW = 128
D_ATTN = 1024
D_KV = 128
N_EXPERT_GROUPS = 8
EXPERTS_PER_GROUP = 8
N_EXPERTS = 64
D_EXPERT = 512
N_MOD = 6
RMS_EPS = 1e-6
NEG_INF = -1e30

LANES = 128
HEADS_PER_VREG = LANES // HEAD_DIM
PAIRS_PER_KV = GQA_GROUP // HEADS_PER_VREG

ADA_TN = 1024
INPROJ_TM = 512
MIX_TQ = 256
MOE_BLK = 256
PAD_PIECES = tuple(MOE_BLK >> (k + 1) for k in range(MOE_BLK.bit_length() - 1))
DISPATCH_TC = 1024
COMBINE_TK = 256
DMA_UNROLL = 8
ROUTER_LANES = 128
VMEM_LIMIT = 56 * 1024 * 1024


def _silu(v):
    return v * jax.nn.sigmoid(v)


def _bf16_bits(v):
    return lax.bitcast_convert_type(v.astype(BF16).astype(F32), U32)


def _load_slabs(ref, rows):
    slabs = ref.shape[0] // rows
    words = [ref[pl.ds(s, rows, stride=slabs), :] for s in range(slabs)]
    lo = [lax.bitcast_convert_type(w << 16, F32).astype(BF16) for w in words]
    hi = [lax.bitcast_convert_type(w & jnp.uint32(0xFFFF0000), F32).astype(BF16) for w in words]
    return lo + hi


def _store_slabs(ref, val):
    rows, d = val.shape
    slabs = ref.shape[0] // rows
    for s in range(slabs):
        lo = _bf16_bits(val[:, s * LANES:(s + 1) * LANES])
        hi = _bf16_bits(val[:, d // 2 + s * LANES:d // 2 + (s + 1) * LANES])
        ref[pl.ds(s, rows, stride=slabs), :] = (lo >> 16) | hi


def _ada_kernel(c_ref, w_ref, b_ref, o_ref):
    ca = _silu(c_ref[...])
    o_ref[...] = jnp.dot(ca.astype(BF16), w_ref[...].astype(BF16),
                         preferred_element_type=F32) + b_ref[...]


def _ada(c, w, b):
    bsz, d = c.shape
    n = w.shape[1]
    return pl.pallas_call(
        _ada_kernel,
        out_shape=jax.ShapeDtypeStruct((bsz, n), F32),
        grid=(n // ADA_TN,),
        in_specs=[
            pl.BlockSpec((bsz, d), lambda i: (0, 0)),
            pl.BlockSpec((d, ADA_TN), lambda i: (0, i)),
            pl.BlockSpec((1, ADA_TN), lambda i: (0, i)),
        ],
        out_specs=pl.BlockSpec((bsz, ADA_TN), lambda i: (0, i)),
        compiler_params=pltpu.CompilerParams(
            dimension_semantics=("arbitrary",), vmem_limit_bytes=VMEM_LIMIT),
        name="ada",
    )(c, w, b.reshape(1, n))


def _norm_mod(x, g, shift, scale):
    ms = jnp.mean(x * x, axis=-1, keepdims=True)
    return (x * lax.rsqrt(ms + RMS_EPS)) * g * (1.0 + scale) + shift


def _inproj_kernel(x_ref, mod_ref, g_ref, w_ref, o_ref):
    h = _norm_mod(x_ref[...], g_ref[...], mod_ref[0, 0:1, :], mod_ref[0, 1:2, :])
    o_ref[...] = jnp.dot(h.astype(BF16), w_ref[...],
                         preferred_element_type=F32).astype(o_ref.dtype)


def _inproj(x2, mod3, g1, w_in, seq):
    t, d = x2.shape
    n = w_in.shape[1]
    per_b = seq // INPROJ_TM
    return pl.pallas_call(
        _inproj_kernel,
        out_shape=jax.ShapeDtypeStruct((t, n), BF16),
        grid=(t // INPROJ_TM,),
        in_specs=[
            pl.BlockSpec((INPROJ_TM, d), lambda i: (i, 0)),
            pl.BlockSpec((1, N_MOD, d), lambda i: (i // per_b, 0, 0)),
            pl.BlockSpec((1, d), lambda i: (0, 0)),
            pl.BlockSpec((d, n), lambda i: (0, 0)),
        ],
        out_specs=pl.BlockSpec((INPROJ_TM, n), lambda i: (i, 0)),
        compiler_params=pltpu.CompilerParams(
            dimension_semantics=("arbitrary",), vmem_limit_bytes=VMEM_LIMIT),
        name="inproj",
    )(x2, mod3, g1, w_in)


def _mix_kernel(sinks_ref, proj_ref, kvp_ref, up_ref, x_ref, mod_ref, wpool_ref,
                pscale_ref, wout_ref, g2_ref, wr_ref, br_ref,
                x1_ref, h2_ref, mi_ref, mf_ref, cnt_ref,
                ubuf, mixbuf, carry):
    b = pl.program_id(0)
    j = pl.program_id(1)
    tq = x_ref.shape[0]
    seq_start = j == 0

    @pl.when(jnp.logical_and(b == 0, j == 0))
    def _():
        carry[...] = jnp.zeros_like(carry)

    halo = up_ref[...].astype(F32)
    ubuf[0:MAX_POOL_WINDOW, :] = jnp.where(seq_start, 0.0, halo)
    ubuf[MAX_POOL_WINDOW:MAX_POOL_WINDOW + tq, :] = proj_ref[:, 0:D_POOL].astype(F32)
    pos = j * tq + lax.broadcasted_iota(I32, (tq, 1), 0)
    for gi, w in enumerate(POOL_WINDOWS):
        c0 = gi * POOL_GROUP
        u = ubuf[MAX_POOL_WINDOW:MAX_POOL_WINDOW + tq, c0:c0 + POOL_GROUP]
        acc = u
        for k in range(1, w):
            acc = acc + ubuf[MAX_POOL_WINDOW - k:MAX_POOL_WINDOW - k + tq, c0:c0 + POOL_GROUP]
        cnt = jnp.minimum(pos + 1, w).astype(F32)
        delta = (acc / cnt - u).astype(BF16)
        yp = jnp.dot(delta, wpool_ref[gi], preferred_element_type=F32)
        yp = yp * pscale_ref[:, c0:c0 + POOL_GROUP]
        mixbuf[:, c0:c0 + POOL_GROUP] = yp.astype(BF16)

    lane = lax.broadcasted_iota(I32, (2 * WINDOW, LANES), 1)
    low = lane < HEAD_DIM
    qi = lax.broadcasted_iota(I32, (WINDOW, 2 * WINDOW), 0)
    kj = lax.broadcasted_iota(I32, (WINDOW, 2 * WINDOW), 1)
    dist = qi - kj + WINDOW
    band = jnp.logical_and(dist >= 0, dist < WINDOW)
    olane = lax.broadcasted_iota(I32, (WINDOW, LANES), 1)
    k_col = D_POOL + D_ATTN
    v_col = k_col + D_KV
    for blk in range(tq // WINDOW):
        r0 = blk * WINDOW
        if blk == 0:
            kcat = jnp.concatenate(
                [kvp_ref[:, 0:D_KV], proj_ref[0:WINDOW, k_col:k_col + D_KV]], axis=0)
            vcat = jnp.concatenate(
                [kvp_ref[:, D_KV:2 * D_KV], proj_ref[0:WINDOW, v_col:v_col + D_KV]], axis=0)
            mask = jnp.logical_and(band, kj >= jnp.where(seq_start, WINDOW, 0))
        else:
            kcat = proj_ref[r0 - WINDOW:r0 + WINDOW, k_col:k_col + D_KV]
            vcat = proj_ref[r0 - WINDOW:r0 + WINDOW, v_col:v_col + D_KV]
            mask = band
        kswap = pltpu.roll(kcat.astype(F32), HEAD_DIM, 1).astype(BF16)
        vswap = pltpu.roll(vcat.astype(F32), HEAD_DIM, 1).astype(BF16)
        zero = jnp.zeros_like(kcat)
        for g in range(N_KV_HEADS):
            ksrc_lo, ksrc_hi = (kcat, kswap) if g == 0 else (kswap, kcat)
            vsrc_lo, vsrc_hi = (vcat, vswap) if g == 0 else (vswap, vcat)
            kbd = jnp.concatenate([jnp.where(low, ksrc_lo, zero),
                                   jnp.where(low, zero, ksrc_hi)], axis=0)
            vbd = jnp.concatenate([jnp.where(low, vsrc_lo, zero),
                                   jnp.where(low, zero, vsrc_hi)], axis=0)
            q_col = D_POOL + g * GQA_GROUP * HEAD_DIM
            q = jnp.concatenate(
                [proj_ref[r0:r0 + WINDOW, q_col + p * LANES:q_col + (p + 1) * LANES]
                 for p in range(PAIRS_PER_KV)], axis=0)
            q = q * jnp.asarray(HEAD_DIM ** -0.5, BF16)
            s = lax.dot_general(q, kbd, (((1,), (1,)), ((), ())),
                                preferred_element_type=F32)
            probs = []
            rdens = []
            for p in range(PAIRS_PER_KV):
                row_p = []
                row_r = []
                for hh in range(HEADS_PER_VREG):
                    sink = sinks_ref[g * GQA_GROUP + p * HEADS_PER_VREG + hh]
                    sp = s[p * WINDOW:(p + 1) * WINDOW,
                           hh * 2 * WINDOW:(hh + 1) * 2 * WINDOW]
                    sp = jnp.where(mask, sp, NEG_INF)
                    m = jnp.maximum(jnp.max(sp, axis=-1, keepdims=True), sink)
                    e = jnp.exp(sp - m)
                    den = jnp.sum(e, axis=-1, keepdims=True) + jnp.exp(sink - m)
                    row_p.append(e.astype(BF16))
                    row_r.append(1.0 / den)
                probs.append(jnp.concatenate(row_p, axis=1))
                rdens.append(row_r)
            pmat = jnp.concatenate(probs, axis=0)
            o = jnp.dot(pmat, vbd, preferred_element_type=F32)
            for p in range(PAIRS_PER_KV):
                op = o[p * WINDOW:(p + 1) * WINDOW, :]
                norm = jnp.where(olane < HEAD_DIM, rdens[p][0], rdens[p][1])
                c0 = D_POOL + g * GQA_GROUP * HEAD_DIM + p * LANES
                mixbuf[r0:r0 + WINDOW, c0:c0 + LANES] = (op * norm).astype(BF16)

    y = jnp.dot(mixbuf[...], wout_ref[...], preferred_element_type=F32)
    x1 = x_ref[...] + mod_ref[0, 2:3, :] * y
    x1_ref[...] = x1

    h2 = _norm_mod(x1, g2_ref[...], mod_ref[0, 3:4, :], mod_ref[0, 4:5, :])
    _store_slabs(h2_ref, h2)

    logits = jnp.dot(h2.astype(BF16), wr_ref[...], preferred_element_type=F32) + br_ref[...]
    ln = lax.broadcasted_iota(I32, (tq, ROUTER_LANES), 1)
    lnf = ln.astype(F32)
    ninf = -jnp.inf
    is_g = ln < N_EXPERT_GROUPS
    gl = jnp.where(is_g, logits, ninf)
    gmax = jnp.max(gl, axis=-1, keepdims=True)
    g_sel = jnp.min(jnp.where(gl == gmax, lnf, float(ROUTER_LANES)), axis=-1, keepdims=True)
    p_grp = 1.0 / jnp.sum(jnp.where(is_g, jnp.exp(logits - gmax), 0.0), axis=-1, keepdims=True)
    lane_grp = ((ln - N_EXPERT_GROUPS) >> 3).astype(F32)
    in_sel = jnp.logical_and(
        jnp.logical_and(ln >= N_EXPERT_GROUPS, ln < N_EXPERT_GROUPS + N_EXPERTS),
        lane_grp == g_sel)
    el = jnp.where(in_sel, logits, ninf)
    l1 = jnp.max(el, axis=-1, keepdims=True)
    i1 = jnp.min(jnp.where(el == l1, lnf, float(ROUTER_LANES)), axis=-1, keepdims=True)
    el2 = jnp.where(lnf == i1, ninf, el)
    l2 = jnp.max(el2, axis=-1, keepdims=True)
    i2 = jnp.min(jnp.where(el2 == l2, lnf, float(ROUTER_LANES)), axis=-1, keepdims=True)
    tt = jnp.exp(l2 - l1)
    w1 = 1.0 / (1.0 + tt)
    comb1 = p_grp * w1
    comb2 = p_grp * (tt * w1)
    hit1 = lnf == i1
    hit2 = lnf == i2
    onehot = jnp.where(jnp.logical_or(hit1, hit2), 1.0, 0.0)
    rr = lax.broadcasted_iota(I32, (tq, tq), 0)
    cc = lax.broadcasted_iota(I32, (tq, tq), 1)
    lower = jnp.where(rr > cc, 1.0, 0.0).astype(BF16)
    prior = jnp.dot(lower, onehot.astype(BF16), preferred_element_type=F32) + carry[...]
    rank1 = jnp.sum(jnp.where(hit1, prior, 0.0), axis=-1, keepdims=True)
    rank2 = jnp.sum(jnp.where(hit2, prior, 0.0), axis=-1, keepdims=True)
    carry[...] = carry[...] + jnp.sum(onehot, axis=0, keepdims=True)
    cnt_ref[...] = carry[...].astype(I32)
    eid1 = i1 - float(N_EXPERT_GROUPS)
    eid2 = i2 - float(N_EXPERT_GROUPS)
    meta = jnp.where(ln == 0, eid1, jnp.where(ln == 1, eid2,
                     jnp.where(ln == 2, rank1, jnp.where(ln == 3, rank2, 0.0))))
    mi_ref[...] = meta.astype(I32)
    mf_ref[...] = jnp.where(ln == 0, comb1, jnp.where(ln == 1, comb2, 0.0))


def _mix(sinks, proj, x2, mod3, w_pool, pool_scale, w_out, g2, w_r, b_r, bsz, seq):
    t, d = x2.shape
    n_in = proj.shape[1]
    per_b = seq // MIX_TQ
    q_per_win = MIX_TQ // WINDOW
    q_per_halo = MIX_TQ // MAX_POOL_WINDOW
    slabs = d // (2 * LANES)

    def row(b, j):
        return b * per_b + j

    out_shapes = (
        jax.ShapeDtypeStruct((t, d), F32),
        jax.ShapeDtypeStruct((t * slabs, LANES), U32),
        jax.ShapeDtypeStruct((t, ROUTER_LANES), I32),
        jax.ShapeDtypeStruct((t, ROUTER_LANES), F32),
        jax.ShapeDtypeStruct((1, ROUTER_LANES), I32),
    )
    return pl.pallas_call(
        _mix_kernel,
        out_shape=out_shapes,
        grid=(bsz, per_b),
        in_specs=[
            pl.BlockSpec(memory_space=pltpu.SMEM),
            pl.BlockSpec((MIX_TQ, n_in), lambda b, j: (row(b, j), 0)),
            pl.BlockSpec((WINDOW, 2 * D_KV),
                         lambda b, j: (jnp.maximum(row(b, j) * q_per_win - 1, 0),
                                       (D_POOL + D_ATTN) // (2 * D_KV))),
            pl.BlockSpec((MAX_POOL_WINDOW, D_POOL),
                         lambda b, j: (jnp.maximum(row(b, j) * q_per_halo - 1, 0), 0)),
            pl.BlockSpec((MIX_TQ, d), lambda b, j: (row(b, j), 0)),
            pl.BlockSpec((1, N_MOD, d), lambda b, j: (b, 0, 0)),
            pl.BlockSpec(w_pool.shape, lambda b, j: (0, 0, 0)),
            pl.BlockSpec((1, D_POOL), lambda b, j: (0, 0)),
            pl.BlockSpec((d, d), lambda b, j: (0, 0)),
            pl.BlockSpec((1, d), lambda b, j: (0, 0)),
            pl.BlockSpec((d, ROUTER_LANES), lambda b, j: (0, 0)),
            pl.BlockSpec((1, ROUTER_LANES), lambda b, j: (0, 0)),
        ],
        out_specs=(
            pl.BlockSpec((MIX_TQ, d), lambda b, j: (row(b, j), 0)),
            pl.BlockSpec((MIX_TQ * slabs, LANES), lambda b, j: (row(b, j), 0)),
            pl.BlockSpec((MIX_TQ, ROUTER_LANES), lambda b, j: (row(b, j), 0)),
            pl.BlockSpec((MIX_TQ, ROUTER_LANES), lambda b, j: (row(b, j), 0)),
            pl.BlockSpec((1, ROUTER_LANES), lambda b, j: (0, 0)),
        ),
        scratch_shapes=[
            pltpu.VMEM((MAX_POOL_WINDOW + MIX_TQ, D_POOL), F32),
            pltpu.VMEM((MIX_TQ, d), BF16),
            pltpu.VMEM((1, ROUTER_LANES), F32),
        ],
        compiler_params=pltpu.CompilerParams(
            dimension_semantics=("arbitrary", "arbitrary"), vmem_limit_bytes=VMEM_LIMIT),
        name="mix",
    )(sinks, proj, proj, proj, x2, mod3, w_pool, pool_scale, w_out, g2, w_r, b_r)


def _dispatch_kernel(pstart_ref, count_ref, e1_ref, e2_ref, r1_ref, r2_ref, h2_ref,
                     xr_hbm, d1_ref, d2_ref, zbuf, sem, zsem):
    tc = e1_ref.shape[0]
    step = pl.program_id(0)
    nblk = xr_hbm.shape[0] // MOE_BLK
    last = N_EXPERTS - 1
    n_used = (pstart_ref[last] + count_ref[last] + MOE_BLK - 1) // MOE_BLK

    def pad_copies(e):
        cnt = count_ref[e]
        npad = (-cnt) & (MOE_BLK - 1)
        off = pstart_ref[e] + cnt
        out = []
        for piece in PAD_PIECES:
            out.append((npad & piece, pltpu.make_async_copy(
                zbuf.at[pl.ds(0, piece)], xr_hbm.at[pl.ds(off, piece)], zsem)))
            off = off + (npad & piece)
        return out

    def tail_copy(blk):
        return pltpu.make_async_copy(zbuf, xr_hbm.at[pl.ds(blk * MOE_BLK, MOE_BLK)], zsem)

    def zero_fill(start):
        def per_expert(e, carry):
            for flag, cp in pad_copies(e):
                pl.when(flag != 0)(cp.start if start else cp.wait)
            return carry

        def per_tail(blk, carry):
            cp = tail_copy(blk)
            cp.start() if start else cp.wait()
            return carry

        lax.fori_loop(0, N_EXPERTS, per_expert, 0)
        lax.fori_loop(n_used, nblk, per_tail, 0)

    @pl.when(step == 0)
    def _():
        zbuf[...] = jnp.zeros_like(zbuf)
        zero_fill(True)

    def copies(tl):
        src = h2_ref.at[tl]
        return (pltpu.make_async_copy(src, xr_hbm.at[d1_ref[tl]], sem),
                pltpu.make_async_copy(src, xr_hbm.at[d2_ref[tl]], sem))

    def issue(i, carry):
        for u in range(DMA_UNROLL):
            tl = i * DMA_UNROLL + u
            d1_ref[tl] = pstart_ref[e1_ref[tl]] + r1_ref[tl]
            d2_ref[tl] = pstart_ref[e2_ref[tl]] + r2_ref[tl]
            for cp in copies(tl):
                cp.start()
        return carry

    def drain(i, carry):
        for u in range(DMA_UNROLL):
            for cp in copies(i * DMA_UNROLL + u):
                cp.wait()
        return carry

    lax.fori_loop(0, tc // DMA_UNROLL, issue, 0)
    lax.fori_loop(0, tc // DMA_UNROLL, drain, 0)

    @pl.when(step == 0)
    def _():
        zero_fill(False)


def _dispatch(pstarts, counts, e1, e2, r1, r2, h2r, n_rows):
    t = e1.shape[0]
    tc = min(DISPATCH_TC, t)
    smem_blk = pl.BlockSpec((tc,), lambda i: (i,), memory_space=pltpu.SMEM)
    return pl.pallas_call(
        _dispatch_kernel,
        out_shape=(
            jax.ShapeDtypeStruct((n_rows,) + h2r.shape[1:], h2r.dtype),
            jax.ShapeDtypeStruct((t,), I32),
            jax.ShapeDtypeStruct((t,), I32),
        ),
        grid=(t // tc,),
        in_specs=[
            pl.BlockSpec(memory_space=pltpu.SMEM),
            pl.BlockSpec(memory_space=pltpu.SMEM),
            smem_blk, smem_blk, smem_blk, smem_blk,
            pl.BlockSpec((tc,) + h2r.shape[1:], lambda i: (i, 0, 0)),
        ],
        out_specs=(pl.BlockSpec(memory_space=pl.ANY), smem_blk, smem_blk),
        scratch_shapes=[pltpu.VMEM((MOE_BLK,) + h2r.shape[1:], h2r.dtype),
                        pltpu.SemaphoreType.DMA(()), pltpu.SemaphoreType.DMA(())],
        compiler_params=pltpu.CompilerParams(
            dimension_semantics=("arbitrary",), has_side_effects=True,
            vmem_limit_bytes=VMEM_LIMIT),
        name="dispatch",
    )(pstarts, counts, e1, e2, r1, r2, h2r)


def _experts_kernel(blk_e_ref, blk_row_ref, blk_valid_ref, blk_first_ref,
                    xr_ref, wg_ref, wu_ref, wd_ref, yr_ref,
                    wg_bf, wu_bf, wd_bf, xb):
    i = pl.program_id(0)
    nvalid = blk_valid_ref[i]

    @pl.when(blk_first_ref[i] == 1)
    def _():
        wg_bf[...] = wg_ref[0].astype(BF16)
        wu_bf[...] = wu_ref[0].astype(BF16)
        wd_bf[...] = wd_ref[0].astype(BF16)

    @pl.when(nvalid > 0)
    def _():
        rows = xb.shape[0]
        for sidx, slab in enumerate(_load_slabs(xr_ref, rows)):
            xb[:, sidx * LANES:(sidx + 1) * LANES] = slab
        xv = xb[...]
        gate = jnp.dot(xv, wg_bf[...], preferred_element_type=F32)
        up = jnp.dot(xv, wu_bf[...], preferred_element_type=F32)
        act = (_silu(gate) * up).astype(BF16)
        _store_slabs(yr_ref, jnp.dot(act, wd_bf[...], preferred_element_type=F32))

    @pl.when(nvalid == 0)
    def _():
        yr_ref[...] = jnp.zeros_like(yr_ref)


def _experts(blk_e, blk_row, blk_valid, blk_first, xr, w_gate, w_up, w_down):
    n_rows, slabs, _ = xr.shape
    d = 2 * slabs * LANES
    nblk = n_rows // MOE_BLK
    grid_spec = pltpu.PrefetchScalarGridSpec(
        num_scalar_prefetch=4,
        grid=(nblk,),
        in_specs=[
            pl.BlockSpec((MOE_BLK * slabs, LANES), lambda i, be, br, bv, bf: (br[i], 0)),
            pl.BlockSpec((1, d, D_EXPERT), lambda i, be, br, bv, bf: (be[i], 0, 0)),
            pl.BlockSpec((1, d, D_EXPERT), lambda i, be, br, bv, bf: (be[i], 0, 0)),
            pl.BlockSpec((1, D_EXPERT, d), lambda i, be, br, bv, bf: (be[i], 0, 0)),
        ],
        out_specs=pl.BlockSpec((MOE_BLK * slabs, LANES), lambda i, be, br, bv, bf: (i, 0)),
        scratch_shapes=[
            pltpu.VMEM((d, D_EXPERT), BF16),
            pltpu.VMEM((d, D_EXPERT), BF16),
            pltpu.VMEM((D_EXPERT, d), BF16),
            pltpu.VMEM((MOE_BLK, d), BF16),
        ],
    )
    yr = pl.pallas_call(
        _experts_kernel,
        out_shape=jax.ShapeDtypeStruct((n_rows * slabs, LANES), U32),
        grid_spec=grid_spec,
        compiler_params=pltpu.CompilerParams(
            dimension_semantics=("arbitrary",), vmem_limit_bytes=VMEM_LIMIT),
        name="experts",
    )(blk_e, blk_row, blk_valid, blk_first, xr.reshape(n_rows * slabs, LANES),
      w_gate, w_up, w_down)
    return yr.reshape(n_rows, slabs, LANES)


def _combine_kernel(d1_ref, d2_ref, yr_hbm, x1_ref, mf_ref, mod_ref, gf_ref, o_ref,
                    ya0, yb0, ya1, yb1, sems):
    i = pl.program_id(0)
    n = pl.num_programs(0)
    tk = x1_ref.shape[0]
    bufs = ((ya0, yb0), (ya1, yb1))

    slabs = yr_hbm.shape[1]

    def copies(tile, slot, tl):
        tok = tile * tk + tl
        ya, yb = bufs[slot]
        dst = pl.ds(pl.multiple_of(tl * slabs, slabs), slabs)
        return (pltpu.make_async_copy(yr_hbm.at[d1_ref[tok]], ya.at[dst], sems.at[slot]),
                pltpu.make_async_copy(yr_hbm.at[d2_ref[tok]], yb.at[dst], sems.at[slot]))

    def issue(tile, slot):
        def body(k, carry):
            for u in range(DMA_UNROLL):
                for cp in copies(tile, slot, k * DMA_UNROLL + u):
                    cp.start()
            return carry
        lax.fori_loop(0, tk // DMA_UNROLL, body, 0)

    def drain(tile, slot):
        def body(k, carry):
            for u in range(DMA_UNROLL):
                for cp in copies(tile, slot, k * DMA_UNROLL + u):
                    cp.wait()
            return carry
        lax.fori_loop(0, tk // DMA_UNROLL, body, 0)

    def step(slot):
        @pl.when(i == 0)
        def _():
            issue(i, slot)

        @pl.when(i + 1 < n)
        def _():
            issue(i + 1, 1 - slot)

        drain(i, slot)
        ya, yb = bufs[slot]
        c1 = mf_ref[:, 0:1]
        c2 = mf_ref[:, 1:2]
        moe = jnp.concatenate(
            [c1 * sa.astype(F32) + c2 * sb.astype(F32)
             for sa, sb in zip(_load_slabs(ya, tk), _load_slabs(yb, tk))], axis=1)
        xo = x1_ref[...] + mod_ref[0, 5:6, :] * moe
        ms = jnp.mean(xo * xo, axis=-1, keepdims=True)
        o_ref[...] = (xo * lax.rsqrt(ms + RMS_EPS)) * gf_ref[...]

    @pl.when(i % 2 == 0)
    def _():
        step(0)

    @pl.when(i % 2 == 1)
    def _():
        step(1)


def _combine(d1, d2, yr, x1, mf, mod3, g_final, seq):
    t, d = x1.shape
    slabs = yr.shape[1]
    per_b = seq // COMBINE_TK
    grid_spec = pltpu.PrefetchScalarGridSpec(
        num_scalar_prefetch=2,
        grid=(t // COMBINE_TK,),
        in_specs=[
            pl.BlockSpec(memory_space=pl.ANY),
            pl.BlockSpec((COMBINE_TK, d), lambda i, a, b: (i, 0)),
            pl.BlockSpec((COMBINE_TK, ROUTER_LANES), lambda i, a, b: (i, 0)),
            pl.BlockSpec((1, N_MOD, d), lambda i, a, b: (i // per_b, 0, 0)),
            pl.BlockSpec((1, d), lambda i, a, b: (0, 0)),
        ],
        out_specs=pl.BlockSpec((COMBINE_TK, d), lambda i, a, b: (i, 0)),
        scratch_shapes=[pltpu.VMEM((COMBINE_TK * slabs, LANES), U32) for _ in range(4)]
        + [pltpu.SemaphoreType.DMA((2,))],
    )
    return pl.pallas_call(
        _combine_kernel,
        out_shape=jax.ShapeDtypeStruct((t, d), F32),
        grid_spec=grid_spec,
        compiler_params=pltpu.CompilerParams(
            dimension_semantics=("arbitrary",), vmem_limit_bytes=VMEM_LIMIT),
        name="combine",
    )(d1, d2, yr, x1, mf, mod3, g_final)


def _block_schedule(counts, n_rows):
    nblk = n_rows // MOE_BLK
    pcounts = (counts + MOE_BLK - 1) // MOE_BLK * MOE_BLK
    pends = jnp.cumsum(pcounts)
    pstarts = pends - pcounts
    blk0 = jnp.arange(nblk, dtype=I32) * MOE_BLK
    n_used = pends[-1] // MOE_BLK
    used = jnp.arange(nblk, dtype=I32) < n_used
    e_of = jnp.minimum(jnp.searchsorted(pends, blk0, side="right"), N_EXPERTS - 1).astype(I32)
    last = jnp.maximum(n_used - 1, 0)
    blk_row = jnp.where(used, jnp.arange(nblk, dtype=I32), last).astype(I32)
    blk_e = jnp.where(used, e_of, e_of[last]).astype(I32)
    valid = jnp.clip(counts[e_of] - (blk0 - pstarts[e_of]), 0, MOE_BLK)
    blk_valid = jnp.where(used, valid, 0).astype(I32)
    prev_e = jnp.concatenate([jnp.full((1,), -1, I32), blk_e[:-1]])
    blk_first = jnp.logical_and(used, blk_e != prev_e).astype(I32)
    return pstarts.astype(I32), blk_e, blk_row, blk_valid, blk_first


def kernel(x, c, w_ada, b_ada, g_norm1, w_in, w_pool, pool_scale, attn_sinks, w_out,
           g_norm2, w_router_group, b_router_group, w_router_expert, b_router_expert,
           w_gate, w_up, w_down, g_final):
    bsz, seq, d = x.shape
    t = bsz * seq
    assert w_ada.shape[0] == 1, "single-layer model: the combine step applies the final norm"
    x2 = x.reshape(t, d)
    for l in range(1):
        mod3 = _ada(c, w_ada[l], b_ada[l]).reshape(bsz, N_MOD, d)
        proj = _inproj(x2, mod3, g_norm1[l].reshape(1, d), w_in[l].astype(BF16), seq)
        pad = ROUTER_LANES - N_EXPERT_GROUPS - N_EXPERTS
        w_r = jnp.concatenate(
            [w_router_group[l], w_router_expert[l], jnp.zeros((d, pad), F32)], axis=1).astype(BF16)
        b_r = jnp.concatenate(
            [b_router_group[l], b_router_expert[l], jnp.zeros((pad,), F32)]).reshape(1, ROUTER_LANES)
        x1, h2r, mi, mf, cnt = _mix(
            attn_sinks[l], proj, x2, mod3, w_pool[l].astype(BF16),
            pool_scale[l].reshape(1, D_POOL), w_out[l].astype(BF16),
            g_norm2[l].reshape(1, d), w_r, b_r, bsz, seq)
        n_rows = 2 * t + N_EXPERTS * MOE_BLK
        counts = cnt[0, N_EXPERT_GROUPS:N_EXPERT_GROUPS + N_EXPERTS]
        pstarts, blk_e, blk_row, blk_valid, blk_first = _block_schedule(counts, n_rows)
        xr, d1, d2 = _dispatch(pstarts, counts, mi[:, 0], mi[:, 1], mi[:, 2], mi[:, 3],
                               h2r.reshape(t, d // (2 * LANES), LANES), n_rows)
        yr = _experts(blk_e, blk_row, blk_valid, blk_first, xr, w_gate[l], w_up[l], w_down[l])
        x2 = _combine(d1, d2, yr, x1, mf, mod3, g_final.reshape(1, d), seq)
    return x2.reshape(bsz, seq, d)
```

```python
import functools

import jax
import jax.numpy as jnp
from jax import lax
from jax.experimental import pallas as pl
from jax.experimental.pallas import tpu as pltpu

F32 = jnp.float32
BF16 = jnp.bfloat16
I32 = jnp.int32
U32 = jnp.uint32

D_POOL = 1024
POOL_WINDOWS = (2, 4, 8, 16)
POOL_GROUP = 256
MAX_POOL_WINDOW = 16
HEAD_DIM = 64
N_KV_HEADS = 2
GQA_GROUP = 8
WINDOW = 128
D_ATTN = 1024
D_KV = 128
N_EXPERT_GROUPS = 8
EXPERTS_PER_GROUP = 8
N_EXPERTS = 64
D_EXPERT = 512
N_MOD = 6
RMS_EPS = 1e-6
NEG_INF = -1e30

LANES = 128
HEADS_PER_VREG = LANES // HEAD_DIM
PAIRS_PER_KV = GQA_GROUP // HEADS_PER_VREG

ADA_TN = 1024
INPROJ_TM = 512
MIX_TQ = 256
MOE_BLK = 256
PAD_PIECES = tuple(MOE_BLK >> (k + 1) for k in range(MOE_BLK.bit_length() - 1))
DISPATCH_TC = 1024
COMBINE_TK = 256
DMA_UNROLL = 8
ROUTER_LANES = 128
META_ROWS = 8
VMEM_LIMIT = 56 * 1024 * 1024


def _silu(v):
    return v * jax.nn.sigmoid(v)


def _load_slabs(ref, rows):
    slabs = ref.shape[0] // rows
    words = [ref[pl.ds(s, rows, stride=slabs), :] for s in range(slabs)]
    return [pltpu.unpack_elementwise(w, index=half, packed_dtype=BF16, unpacked_dtype=F32)
            for half in range(2) for w in words]


def _store_slabs(ref, val):
    rows, d = val.shape
    slabs = ref.shape[0] // rows
    for s in range(slabs):
        lo = val[:, s * LANES:(s + 1) * LANES]
        hi = val[:, d // 2 + s * LANES:d // 2 + (s + 1) * LANES]
        ref[pl.ds(s, rows, stride=slabs), :] = pltpu.pack_elementwise([lo, hi], packed_dtype=BF16)


def _ada_kernel(c_ref, w_ref, b_ref, o_ref):
    ca = _silu(c_ref[...])
    o_ref[...] = jnp.dot(ca.astype(BF16), w_ref[...].astype(BF16),
                         preferred_element_type=F32) + b_ref[...]


def _ada(c, w, b):
    bsz, d = c.shape
    n = w.shape[1]
    return pl.pallas_call(
        _ada_kernel,
        out_shape=jax.ShapeDtypeStruct((bsz, n), F32),
        grid=(n // ADA_TN,),
        in_specs=[
            pl.BlockSpec((bsz, d), lambda i: (0, 0)),
            pl.BlockSpec((d, ADA_TN), lambda i: (0, i)),
            pl.BlockSpec((1, ADA_TN), lambda i: (0, i)),
        ],
        out_specs=pl.BlockSpec((bsz, ADA_TN), lambda i: (0, i)),
        compiler_params=pltpu.CompilerParams(
            dimension_semantics=("arbitrary",), vmem_limit_bytes=VMEM_LIMIT),
        name="ada",
    )(c, w, b.reshape(1, n))


def _norm_mod(x, g, shift, scale):
    ms = jnp.mean(x * x, axis=-1, keepdims=True)
    return (x * lax.rsqrt(ms + RMS_EPS)) * g * (1.0 + scale) + shift


def _inproj_kernel(x_ref, mod_ref, g_ref, w_ref, o_ref):
    h = _norm_mod(x_ref[...], g_ref[...], mod_ref[0, 0:1, :], mod_ref[0, 1:2, :])
    o_ref[...] = jnp.dot(h.astype(BF16), w_ref[...],
                         preferred_element_type=F32).astype(o_ref.dtype)


def _inproj(x2, mod3, g1, w_in, seq):
    t, d = x2.shape
    n = w_in.shape[1]
    per_b = seq // INPROJ_TM
    return pl.pallas_call(
        _inproj_kernel,
        out_shape=jax.ShapeDtypeStruct((t, n), BF16),
        grid=(t // INPROJ_TM,),
        in_specs=[
            pl.BlockSpec((INPROJ_TM, d), lambda i: (i, 0)),
            pl.BlockSpec((1, N_MOD, d), lambda i: (i // per_b, 0, 0)),
            pl.BlockSpec((1, d), lambda i: (0, 0)),
            pl.BlockSpec((d, n), lambda i: (0, 0)),
        ],
        out_specs=pl.BlockSpec((INPROJ_TM, n), lambda i: (i, 0)),
        compiler_params=pltpu.CompilerParams(
            dimension_semantics=("arbitrary",), vmem_limit_bytes=VMEM_LIMIT),
        name="inproj",
    )(x2, mod3, g1, w_in)


def _mix_kernel(sinks_ref, proj_ref, kvp_ref, up_ref, x_ref, mod_ref, wpool_ref,
                pscale_ref, wout_ref, g2_ref, wr_ref, br_ref,
                x1_ref, h2_ref, mi_ref, mf_ref, cnt_ref,
                ubuf, mixbuf, carry):
    b = pl.program_id(0)
    j = pl.program_id(1)
    tq = x_ref.shape[0]
    seq_start = j == 0

    @pl.when(jnp.logical_and(b == 0, j == 0))
    def _():
        carry[...] = jnp.zeros_like(carry)

    halo = up_ref[...].astype(F32)
    ubuf[0:MAX_POOL_WINDOW, :] = jnp.where(seq_start, 0.0, halo)
    ubuf[MAX_POOL_WINDOW:MAX_POOL_WINDOW + tq, :] = proj_ref[:, 0:D_POOL].astype(F32)
    pos = j * tq + lax.broadcasted_iota(I32, (tq, 1), 0)
    for gi, w in enumerate(POOL_WINDOWS):
        c0 = gi * POOL_GROUP
        u = ubuf[MAX_POOL_WINDOW:MAX_POOL_WINDOW + tq, c0:c0 + POOL_GROUP]
        acc = u
        for k in range(1, w):
            acc = acc + ubuf[MAX_POOL_WINDOW - k:MAX_POOL_WINDOW - k + tq, c0:c0 + POOL_GROUP]
        cnt = jnp.minimum(pos + 1, w).astype(F32)
        delta = (acc / cnt - u).astype(BF16)
        yp = jnp.dot(delta, wpool_ref[gi], preferred_element_type=F32)
        yp = yp * pscale_ref[:, c0:c0 + POOL_GROUP]
        mixbuf[:, c0:c0 + POOL_GROUP] = yp.astype(BF16)

    lane = lax.broadcasted_iota(I32, (2 * WINDOW, LANES), 1)
    low = lane < HEAD_DIM
    qi = lax.broadcasted_iota(I32, (WINDOW, 2 * WINDOW), 0)
    kj = lax.broadcasted_iota(I32, (WINDOW, 2 * WINDOW), 1)
    dist = qi - kj + WINDOW
    band = jnp.logical_and(dist >= 0, dist < WINDOW)
    olane = lax.broadcasted_iota(I32, (WINDOW, LANES), 1)
    k_col = D_POOL + D_ATTN
    v_col = k_col + D_KV
    for blk in range(tq // WINDOW):
        r0 = blk * WINDOW
        if blk == 0:
            kcat = jnp.concatenate(
                [kvp_ref[:, 0:D_KV], proj_ref[0:WINDOW, k_col:k_col + D_KV]], axis=0)
            vcat = jnp.concatenate(
                [kvp_ref[:, D_KV:2 * D_KV], proj_ref[0:WINDOW, v_col:v_col + D_KV]], axis=0)
            mask = jnp.logical_and(band, kj >= jnp.where(seq_start, WINDOW, 0))
        else:
            kcat = proj_ref[r0 - WINDOW:r0 + WINDOW, k_col:k_col + D_KV]
            vcat = proj_ref[r0 - WINDOW:r0 + WINDOW, v_col:v_col + D_KV]
            mask = band
        kswap = pltpu.roll(kcat.astype(F32), HEAD_DIM, 1).astype(BF16)
        vswap = pltpu.roll(vcat.astype(F32), HEAD_DIM, 1).astype(BF16)
        zero = jnp.zeros_like(kcat)
        for g in range(N_KV_HEADS):
            ksrc_lo, ksrc_hi = (kcat, kswap) if g == 0 else (kswap, kcat)
            vsrc_lo, vsrc_hi = (vcat, vswap) if g == 0 else (vswap, vcat)
            kbd = jnp.concatenate([jnp.where(low, ksrc_lo, zero),
                                   jnp.where(low, zero, ksrc_hi)], axis=0)
            vbd = jnp.concatenate([jnp.where(low, vsrc_lo, zero),
                                   jnp.where(low, zero, vsrc_hi)], axis=0)
            q_col = D_POOL + g * GQA_GROUP * HEAD_DIM
            q = jnp.concatenate(
                [proj_ref[r0:r0 + WINDOW, q_col + p * LANES:q_col + (p + 1) * LANES]
                 for p in range(PAIRS_PER_KV)], axis=0)
            q = q * jnp.asarray(HEAD_DIM ** -0.5, BF16)
            s = lax.dot_general(q, kbd, (((1,), (1,)), ((), ())),
                                preferred_element_type=F32)
            probs = []
            rdens = []
            for p in range(PAIRS_PER_KV):
                row_p = []
                row_r = []
                for hh in range(HEADS_PER_VREG):
                    sink = sinks_ref[g * GQA_GROUP + p * HEADS_PER_VREG + hh]
                    sp = s[p * WINDOW:(p + 1) * WINDOW,
                           hh * 2 * WINDOW:(hh + 1) * 2 * WINDOW]
                    sp = jnp.where(mask, sp, NEG_INF)
                    m = jnp.maximum(jnp.max(sp, axis=-1, keepdims=True), sink)
                    e = jnp.exp(sp - m)
                    den = jnp.sum(e, axis=-1, keepdims=True) + jnp.exp(sink - m)
                    row_p.append(e.astype(BF16))
                    row_r.append(1.0 / den)
                probs.append(jnp.concatenate(row_p, axis=1))
                rdens.append(row_r)
            pmat = jnp.concatenate(probs, axis=0)
            o = jnp.dot(pmat, vbd, preferred_element_type=F32)
            for p in range(PAIRS_PER_KV):
                op = o[p * WINDOW:(p + 1) * WINDOW, :]
                norm = jnp.where(olane < HEAD_DIM, rdens[p][0], rdens[p][1])
                c0 = D_POOL + g * GQA_GROUP * HEAD_DIM + p * LANES
                mixbuf[r0:r0 + WINDOW, c0:c0 + LANES] = (op * norm).astype(BF16)

    y = jnp.dot(mixbuf[...], wout_ref[...], preferred_element_type=F32)
    x1 = x_ref[...] + mod_ref[0, 2:3, :] * y
    x1_ref[...] = x1

    h2 = _norm_mod(x1, g2_ref[...], mod_ref[0, 3:4, :], mod_ref[0, 4:5, :])
    _store_slabs(h2_ref, h2)

    logits = jnp.dot(h2.astype(BF16), wr_ref[...], preferred_element_type=F32) + br_ref[...]
    ln = lax.broadcasted_iota(I32, (tq, ROUTER_LANES), 1)
    lnf = ln.astype(F32)
    ninf = -jnp.inf
    is_g = ln < N_EXPERT_GROUPS
    gl = jnp.where(is_g, logits, ninf)
    gmax = jnp.max(gl, axis=-1, keepdims=True)
    g_sel = jnp.min(jnp.where(gl == gmax, lnf, float(ROUTER_LANES)), axis=-1, keepdims=True)
    p_grp = 1.0 / jnp.sum(jnp.where(is_g, jnp.exp(logits - gmax), 0.0), axis=-1, keepdims=True)
    lane_grp = ((ln - N_EXPERT_GROUPS) >> 3).astype(F32)
    in_sel = jnp.logical_and(
        jnp.logical_and(ln >= N_EXPERT_GROUPS, ln < N_EXPERT_GROUPS + N_EXPERTS),
        lane_grp == g_sel)
    el = jnp.where(in_sel, logits, ninf)
    l1 = jnp.max(el, axis=-1, keepdims=True)
    i1 = jnp.min(jnp.where(el == l1, lnf, float(ROUTER_LANES)), axis=-1, keepdims=True)
    el2 = jnp.where(lnf == i1, ninf, el)
    l2 = jnp.max(el2, axis=-1, keepdims=True)
    i2 = jnp.min(jnp.where(el2 == l2, lnf, float(ROUTER_LANES)), axis=-1, keepdims=True)
    tt = jnp.exp(l2 - l1)
    w1 = 1.0 / (1.0 + tt)
    comb1 = p_grp * w1
    comb2 = p_grp * (tt * w1)
    hit1 = lnf == i1
    hit2 = lnf == i2
    onehot = jnp.where(jnp.logical_or(hit1, hit2), 1.0, 0.0)
    rr = lax.broadcasted_iota(I32, (tq, tq), 0)
    cc = lax.broadcasted_iota(I32, (tq, tq), 1)
    lower = jnp.where(rr > cc, 1.0, 0.0).astype(BF16)
    prior = jnp.dot(lower, onehot.astype(BF16), preferred_element_type=F32) + carry[...]
    rank1 = jnp.sum(jnp.where(hit1, prior, 0.0), axis=-1, keepdims=True)
    rank2 = jnp.sum(jnp.where(hit2, prior, 0.0), axis=-1, keepdims=True)
    carry[...] = carry[...] + jnp.sum(onehot, axis=0, keepdims=True)
    cnt_ref[...] = carry[...].astype(I32)
    eid1 = i1 - float(N_EXPERT_GROUPS)
    eid2 = i2 - float(N_EXPERT_GROUPS)
    meta = jnp.where(ln == 0, eid1, jnp.where(ln == 1, eid2,
                     jnp.where(ln == 2, rank1, jnp.where(ln == 3, rank2, 0.0))))
    mi_ref[...] = meta.T[0:META_ROWS, :].astype(I32)
    mf_ref[...] = jnp.where(ln == 0, comb1, jnp.where(ln == 1, comb2, 0.0))


def _mix(sinks, proj, x2, mod3, w_pool, pool_scale, w_out, g2, w_r, b_r, bsz, seq):
    t, d = x2.shape
    n_in = proj.shape[1]
    per_b = seq // MIX_TQ
    q_per_win = MIX_TQ // WINDOW
    q_per_halo = MIX_TQ // MAX_POOL_WINDOW
    slabs = d // (2 * LANES)

    def row(b, j):
        return b * per_b + j

    out_shapes = (
        jax.ShapeDtypeStruct((t, d), F32),
        jax.ShapeDtypeStruct((t * slabs, LANES), U32),
        jax.ShapeDtypeStruct((META_ROWS, t), I32),
        jax.ShapeDtypeStruct((t, ROUTER_LANES), F32),
        jax.ShapeDtypeStruct((1, ROUTER_LANES), I32),
    )
    return pl.pallas_call(
        _mix_kernel,
        out_shape=out_shapes,
        grid=(bsz, per_b),
        in_specs=[
            pl.BlockSpec(memory_space=pltpu.SMEM),
            pl.BlockSpec((MIX_TQ, n_in), lambda b, j: (row(b, j), 0)),
            pl.BlockSpec((WINDOW, 2 * D_KV),
                         lambda b, j: (jnp.maximum(row(b, j) * q_per_win - 1, 0),
                                       (D_POOL + D_ATTN) // (2 * D_KV))),
            pl.BlockSpec((MAX_POOL_WINDOW, D_POOL),
                         lambda b, j: (jnp.maximum(row(b, j) * q_per_halo - 1, 0), 0)),
            pl.BlockSpec((MIX_TQ, d), lambda b, j: (row(b, j), 0)),
            pl.BlockSpec((1, N_MOD, d), lambda b, j: (b, 0, 0)),
            pl.BlockSpec(w_pool.shape, lambda b, j: (0, 0, 0)),
            pl.BlockSpec((1, D_POOL), lambda b, j: (0, 0)),
            pl.BlockSpec((d, d), lambda b, j: (0, 0)),
            pl.BlockSpec((1, d), lambda b, j: (0, 0)),
            pl.BlockSpec((d, ROUTER_LANES), lambda b, j: (0, 0)),
            pl.BlockSpec((1, ROUTER_LANES), lambda b, j: (0, 0)),
        ],
        out_specs=(
            pl.BlockSpec((MIX_TQ, d), lambda b, j: (row(b, j), 0)),
            pl.BlockSpec((MIX_TQ * slabs, LANES), lambda b, j: (row(b, j), 0)),
            pl.BlockSpec((META_ROWS, MIX_TQ), lambda b, j: (0, row(b, j))),
            pl.BlockSpec((MIX_TQ, ROUTER_LANES), lambda b, j: (row(b, j), 0)),
            pl.BlockSpec((1, ROUTER_LANES), lambda b, j: (0, 0)),
        ),
        scratch_shapes=[
            pltpu.VMEM((MAX_POOL_WINDOW + MIX_TQ, D_POOL), F32),
            pltpu.VMEM((MIX_TQ, d), BF16),
            pltpu.VMEM((1, ROUTER_LANES), F32),
        ],
        compiler_params=pltpu.CompilerParams(
            dimension_semantics=("arbitrary", "arbitrary"), vmem_limit_bytes=VMEM_LIMIT),
        name="mix",
    )(sinks, proj, proj, proj, x2, mod3, w_pool, pool_scale, w_out, g2, w_r, b_r)


def _dispatch_kernel(pstart_ref, count_ref, e1_ref, e2_ref, r1_ref, r2_ref, h2_ref,
                     xr_hbm, d1_ref, d2_ref, zbuf, sem, zsem):
    tc = e1_ref.shape[0]
    step = pl.program_id(0)
    nblk = xr_hbm.shape[0] // MOE_BLK
    last = N_EXPERTS - 1
    n_used = (pstart_ref[last] + count_ref[last] + MOE_BLK - 1) // MOE_BLK

    def pad_copies(e):
        cnt = count_ref[e]
        npad = (-cnt) & (MOE_BLK - 1)
        off = pstart_ref[e] + cnt
        out = []
        for piece in PAD_PIECES:
            out.append((npad & piece, pltpu.make_async_copy(
                zbuf.at[pl.ds(0, piece)], xr_hbm.at[pl.ds(off, piece)], zsem)))
            off = off + (npad & piece)
        return out

    def tail_copy(blk):
        return pltpu.make_async_copy(zbuf, xr_hbm.at[pl.ds(blk * MOE_BLK, MOE_BLK)], zsem)

    def zero_fill(start):
        def per_expert(e, carry):
            for flag, cp in pad_copies(e):
                pl.when(flag != 0)(cp.start if start else cp.wait)
            return carry

        def per_tail(blk, carry):
            cp = tail_copy(blk)
            cp.start() if start else cp.wait()
            return carry

        lax.fori_loop(0, N_EXPERTS, per_expert, 0)
        lax.fori_loop(n_used, nblk, per_tail, 0)

    @pl.when(step == 0)
    def _():
        zbuf[...] = jnp.zeros_like(zbuf)
        zero_fill(True)

    def copies(tl):
        src = h2_ref.at[tl]
        return (pltpu.make_async_copy(src, xr_hbm.at[d1_ref[tl]], sem),
                pltpu.make_async_copy(src, xr_hbm.at[d2_ref[tl]], sem))

    def issue(i, carry):
        for u in range(DMA_UNROLL):
            tl = i * DMA_UNROLL + u
            d1_ref[tl] = pstart_ref[e1_ref[tl]] + r1_ref[tl]
            d2_ref[tl] = pstart_ref[e2_ref[tl]] + r2_ref[tl]
            for cp in copies(tl):
                cp.start()
        return carry

    def drain(i, carry):
        for u in range(DMA_UNROLL):
            for cp in copies(i * DMA_UNROLL + u):
                cp.wait()
        return carry

    lax.fori_loop(0, tc // DMA_UNROLL, issue, 0)
    lax.fori_loop(0, tc // DMA_UNROLL, drain, 0)

    @pl.when(step == 0)
    def _():
        zero_fill(False)


def _dispatch(pstarts, counts, e1, e2, r1, r2, h2r, n_rows):
    t = e1.shape[0]
    tc = min(DISPATCH_TC, t)
    smem_blk = pl.BlockSpec((tc,), lambda i: (i,), memory_space=pltpu.SMEM)
    return pl.pallas_call(
        _dispatch_kernel,
        out_shape=(
            jax.ShapeDtypeStruct((n_rows,) + h2r.shape[1:], h2r.dtype),
            jax.ShapeDtypeStruct((t,), I32),
            jax.ShapeDtypeStruct((t,), I32),
        ),
        grid=(t // tc,),
        in_specs=[
            pl.BlockSpec(memory_space=pltpu.SMEM),
            pl.BlockSpec(memory_space=pltpu.SMEM),
            smem_blk, smem_blk, smem_blk, smem_blk,
            pl.BlockSpec((tc,) + h2r.shape[1:], lambda i: (i, 0, 0)),
        ],
        out_specs=(pl.BlockSpec(memory_space=pl.ANY), smem_blk, smem_blk),
        scratch_shapes=[pltpu.VMEM((MOE_BLK,) + h2r.shape[1:], h2r.dtype),
                        pltpu.SemaphoreType.DMA(()), pltpu.SemaphoreType.DMA(())],
        compiler_params=pltpu.CompilerParams(
            dimension_semantics=("arbitrary",), has_side_effects=True,
            vmem_limit_bytes=VMEM_LIMIT),
        name="dispatch",
    )(pstarts, counts, e1, e2, r1, r2, h2r)


def _experts_kernel(blk_e_ref, blk_row_ref, blk_valid_ref, blk_first_ref, blk_slot_ref,
                    blk_next_ref, xr_ref, wg_hbm, wu_hbm, wd_hbm, yr_ref,
                    wg_f32, wu_f32, wd_f32, wg_bf, wu_bf, wd_bf, xb, wsems):
    i = pl.program_id(0)
    nvalid = blk_valid_ref[i]
    slot = blk_slot_ref[i]

    def weight_copies(expert, s):
        return [pltpu.make_async_copy(src.at[expert], dst.at[s], wsems.at[s])
                for src, dst in ((wg_hbm, wg_f32), (wu_hbm, wu_f32), (wd_hbm, wd_f32))]

    @pl.when(i == 0)
    def _():
        for cp in weight_copies(blk_e_ref[0], slot):
            cp.start()

    @pl.when(blk_first_ref[i] == 1)
    def _():
        nxt = blk_next_ref[i]

        @pl.when(nxt >= 0)
        def _():
            for cp in weight_copies(nxt, 1 - slot):
                cp.start()

        for cp in weight_copies(blk_e_ref[i], slot):
            cp.wait()
        wg_bf[...] = wg_f32[slot].astype(BF16)
        wu_bf[...] = wu_f32[slot].astype(BF16)
        wd_bf[...] = wd_f32[slot].astype(BF16)

    @pl.when(nvalid > 0)
    def _():
        rows = xb.shape[0]
        for sidx, slab in enumerate(_load_slabs(xr_ref, rows)):
            xb[:, sidx * LANES:(sidx + 1) * LANES] = slab.astype(BF16)
        xv = xb[...]
        gate = jnp.dot(xv, wg_bf[...], preferred_element_type=F32)
        up = jnp.dot(xv, wu_bf[...], preferred_element_type=F32)
        act = (_silu(gate) * up).astype(BF16)
        _store_slabs(yr_ref, jnp.dot(act, wd_bf[...], preferred_element_type=F32))

    @pl.when(nvalid == 0)
    def _():
        _store_slabs(yr_ref, jnp.zeros((xb.shape[0], xb.shape[1]), F32))


def _experts(schedule, xr, w_gate, w_up, w_down):
    n_rows, slabs, _ = xr.shape
    d = 2 * slabs * LANES
    nblk = n_rows // MOE_BLK
    grid_spec = pltpu.PrefetchScalarGridSpec(
        num_scalar_prefetch=len(schedule),
        grid=(nblk,),
        in_specs=[
            pl.BlockSpec((MOE_BLK * slabs, LANES), lambda i, *sched: (sched[1][i], 0)),
            pl.BlockSpec(memory_space=pl.ANY),
            pl.BlockSpec(memory_space=pl.ANY),
            pl.BlockSpec(memory_space=pl.ANY),
        ],
        out_specs=pl.BlockSpec((MOE_BLK * slabs, LANES), lambda i, *sched: (i, 0)),
        scratch_shapes=[
            pltpu.VMEM((2, d, D_EXPERT), F32),
            pltpu.VMEM((2, d, D_EXPERT), F32),
            pltpu.VMEM((2, D_EXPERT, d), F32),
            pltpu.VMEM((d, D_EXPERT), BF16),
            pltpu.VMEM((d, D_EXPERT), BF16),
            pltpu.VMEM((D_EXPERT, d), BF16),
            pltpu.VMEM((MOE_BLK, d), BF16),
            pltpu.SemaphoreType.DMA((2,)),
        ],
    )
    yr = pl.pallas_call(
        _experts_kernel,
        out_shape=jax.ShapeDtypeStruct((n_rows * slabs, LANES), U32),
        grid_spec=grid_spec,
        compiler_params=pltpu.CompilerParams(
            dimension_semantics=("arbitrary",), vmem_limit_bytes=VMEM_LIMIT),
        name="experts",
    )(*schedule, xr.reshape(n_rows * slabs, LANES), w_gate, w_up, w_down)
    return yr.reshape(n_rows, slabs, LANES)


def _combine_kernel(d1_ref, d2_ref, yr_hbm, x1_ref, mf_ref, mod_ref, gf_ref, o_ref,
                    ya0, yb0, ya1, yb1, sems):
    i = pl.program_id(0)
    n = pl.num_programs(0)
    tk = x1_ref.shape[0]
    bufs = ((ya0, yb0), (ya1, yb1))

    slabs = yr_hbm.shape[1]

    def copies(tile, slot, tl):
        tok = tile * tk + tl
        ya, yb = bufs[slot]
        dst = pl.ds(pl.multiple_of(tl * slabs, slabs), slabs)
        return (pltpu.make_async_copy(yr_hbm.at[d1_ref[tok]], ya.at[dst], sems.at[slot]),
                pltpu.make_async_copy(yr_hbm.at[d2_ref[tok]], yb.at[dst], sems.at[slot]))

    def issue(tile, slot):
        def body(k, carry):
            for u in range(DMA_UNROLL):
                for cp in copies(tile, slot, k * DMA_UNROLL + u):
                    cp.start()
            return carry
        lax.fori_loop(0, tk // DMA_UNROLL, body, 0)

    def drain(tile, slot):
        def body(k, carry):
            for u in range(DMA_UNROLL):
                for cp in copies(tile, slot, k * DMA_UNROLL + u):
                    cp.wait()
            return carry
        lax.fori_loop(0, tk // DMA_UNROLL, body, 0)

    def step(slot):
        @pl.when(i == 0)
        def _():
            issue(i, slot)

        @pl.when(i + 1 < n)
        def _():
            issue(i + 1, 1 - slot)

        drain(i, slot)
        ya, yb = bufs[slot]
        c1 = mf_ref[:, 0:1]
        c2 = mf_ref[:, 1:2]
        moe = jnp.concatenate(
            [c1 * sa + c2 * sb for sa, sb in zip(_load_slabs(ya, tk), _load_slabs(yb, tk))], axis=1)
        xo = x1_ref[...] + mod_ref[0, 5:6, :] * moe
        ms = jnp.mean(xo * xo, axis=-1, keepdims=True)
        o_ref[...] = (xo * lax.rsqrt(ms + RMS_EPS)) * gf_ref[...]

    @pl.when(i % 2 == 0)
    def _():
        step(0)

    @pl.when(i % 2 == 1)
    def _():
        step(1)


def _combine(d1, d2, yr, x1, mf, mod3, g_final, seq):
    t, d = x1.shape
    slabs = yr.shape[1]
    per_b = seq // COMBINE_TK
    grid_spec = pltpu.PrefetchScalarGridSpec(
        num_scalar_prefetch=2,
        grid=(t // COMBINE_TK,),
        in_specs=[
            pl.BlockSpec(memory_space=pl.ANY),
            pl.BlockSpec((COMBINE_TK, d), lambda i, a, b: (i, 0)),
            pl.BlockSpec((COMBINE_TK, ROUTER_LANES), lambda i, a, b: (i, 0)),
            pl.BlockSpec((1, N_MOD, d), lambda i, a, b: (i // per_b, 0, 0)),
            pl.BlockSpec((1, d), lambda i, a, b: (0, 0)),
        ],
        out_specs=pl.BlockSpec((COMBINE_TK, d), lambda i, a, b: (i, 0)),
        scratch_shapes=[pltpu.VMEM((COMBINE_TK * slabs, LANES), U32) for _ in range(4)]
        + [pltpu.SemaphoreType.DMA((2,))],
    )
    return pl.pallas_call(
        _combine_kernel,
        out_shape=jax.ShapeDtypeStruct((t, d), F32),
        grid_spec=grid_spec,
        compiler_params=pltpu.CompilerParams(
            dimension_semantics=("arbitrary",), vmem_limit_bytes=VMEM_LIMIT),
        name="combine",
    )(d1, d2, yr, x1, mf, mod3, g_final)


def _block_schedule(counts, n_rows):
    nblk = n_rows // MOE_BLK
    eidx = jnp.arange(N_EXPERTS, dtype=I32)
    bidx = jnp.arange(nblk, dtype=I32)
    nblocks = (counts + MOE_BLK - 1) // MOE_BLK
    bends = jnp.cumsum(nblocks)
    bstarts = bends - nblocks
    n_used = bends[-1]
    used = bidx < n_used
    blk_row = jnp.minimum(bidx, jnp.maximum(n_used - 1, 0))
    owner = jnp.logical_and(blk_row[:, None] >= bstarts[None, :], blk_row[:, None] < bends[None, :])

    def pick(per_expert):
        return jnp.sum(jnp.where(owner, per_expert[None, :], 0), axis=1).astype(I32)

    blk_e = pick(eidx)
    blk_valid = jnp.where(
        used, jnp.clip(pick(counts) - (bidx - pick(bstarts)) * MOE_BLK, 0, MOE_BLK), 0).astype(I32)
    blk_first = jnp.logical_and(used, bidx == pick(bstarts)).astype(I32)
    nonempty = counts > 0
    blk_slot = pick(jnp.cumsum(nonempty.astype(I32)) - 1) & 1
    later = jnp.logical_and(nonempty[None, :], eidx[None, :] > eidx[:, None])
    nxt = jnp.min(jnp.where(later, eidx[None, :], N_EXPERTS), axis=1)
    blk_next = pick(jnp.where(nxt < N_EXPERTS, nxt, -1))
    pstarts = (bstarts * MOE_BLK).astype(I32)
    return pstarts, (blk_e, blk_row, blk_valid, blk_first, blk_slot, blk_next)


def kernel(x, c, w_ada, b_ada, g_norm1, w_in, w_pool, pool_scale, attn_sinks, w_out,
           g_norm2, w_router_group, b_router_group, w_router_expert, b_router_expert,
           w_gate, w_up, w_down, g_final):
    bsz, seq, d = x.shape
    t = bsz * seq
    assert w_ada.shape[0] == 1, "single-layer model: the combine step applies the final norm"
    x2 = x.reshape(t, d)
    for l in range(1):
        mod3 = _ada(c, w_ada[l], b_ada[l]).reshape(bsz, N_MOD, d)
        proj = _inproj(x2, mod3, g_norm1[l].reshape(1, d), w_in[l].astype(BF16), seq)
        pad = ROUTER_LANES - N_EXPERT_GROUPS - N_EXPERTS
        w_r = jnp.concatenate(
            [w_router_group[l], w_router_expert[l], jnp.zeros((d, pad), F32)], axis=1).astype(BF16)
        b_r = jnp.concatenate(
            [b_router_group[l], b_router_expert[l], jnp.zeros((pad,), F32)]).reshape(1, ROUTER_LANES)
        x1, h2r, mi, mf, cnt = _mix(
            attn_sinks[l], proj, x2, mod3, w_pool[l].astype(BF16),
            pool_scale[l].reshape(1, D_POOL), w_out[l].astype(BF16),
            g_norm2[l].reshape(1, d), w_r, b_r, bsz, seq)
        n_rows = 2 * t + N_EXPERTS * MOE_BLK
        counts = cnt[0, N_EXPERT_GROUPS:N_EXPERT_GROUPS + N_EXPERTS]
        pstarts, schedule = _block_schedule(counts, n_rows)
        xr, d1, d2 = _dispatch(pstarts, counts, mi[0], mi[1], mi[2], mi[3],
                               h2r.reshape(t, d // (2 * LANES), LANES), n_rows)
        yr = _experts(schedule, xr, w_gate[l], w_up[l], w_down[l])
        x2 = _combine(d1, d2, yr, x1, mf, mod3, g_final.reshape(1, d), seq)
    return x2.reshape(bsz, seq, d)
```

```python
import functools

import jax
import jax.numpy as jnp
from jax import lax
from jax.experimental import pallas as pl
from jax.experimental.pallas import tpu as pltpu

F32 = jnp.float32
BF16 = jnp.bfloat16
I32 = jnp.int32
U32 = jnp.uint32

D_POOL = 1024
POOL_WINDOWS = (2, 4, 8, 16)
POOL_GROUP = 256
MAX_POOL_WINDOW = 16
HEAD_DIM = 64
N_KV_HEADS = 2
GQA_GROUP = 8
WINDOW = 128
D_ATTN = 1024
D_KV = 128
N_EXPERT_GROUPS = 8
EXPERTS_PER_GROUP = 8
N_EXPERTS = 64
D_EXPERT = 512
N_MOD = 6
RMS_EPS = 1e-6
NEG_INF = -1e30

LANES = 128
HEADS_PER_VREG = LANES // HEAD_DIM
PAIRS_PER_KV = GQA_GROUP // HEADS_PER_VREG

ADA_TN = 1024
INPROJ_TM = 512
MIX_TQ = 256
MOE_BLK = 256
PAD_PIECES = tuple(MOE_BLK >> (k + 1) for k in range(MOE_BLK.bit_length() - 1))
DISPATCH_TC = 1024
COMBINE_TK = 256
DMA_UNROLL = 8
DISPATCH_UNROLL = 64
ROUTER_LANES = 128
META_ROWS = 8
VMEM_LIMIT = 56 * 1024 * 1024


def _silu(v):
    return v * jax.nn.sigmoid(v)


def _load_slabs(ref, rows):
    slabs = ref.shape[0] // rows
    words = [ref[pl.ds(s, rows, stride=slabs), :] for s in range(slabs)]
    return [pltpu.unpack_elementwise(w, index=half, packed_dtype=BF16, unpacked_dtype=F32)
            for half in range(2) for w in words]


def _store_slabs(ref, val):
    rows, d = val.shape
    slabs = ref.shape[0] // rows
    for s in range(slabs):
        lo = val[:, s * LANES:(s + 1) * LANES]
        hi = val[:, d // 2 + s * LANES:d // 2 + (s + 1) * LANES]
        ref[pl.ds(s, rows, stride=slabs), :] = pltpu.pack_elementwise([lo, hi], packed_dtype=BF16)


def _ada_kernel(c_ref, w_ref, b_ref, o_ref):
    ca = _silu(c_ref[...])
    o_ref[...] = jnp.dot(ca.astype(BF16), w_ref[...].astype(BF16),
                         preferred_element_type=F32) + b_ref[...]


def _ada(c, w, b):
    bsz, d = c.shape
    n = w.shape[1]
    return pl.pallas_call(
        _ada_kernel,
        out_shape=jax.ShapeDtypeStruct((bsz, n), F32),
        grid=(n // ADA_TN,),
        in_specs=[
            pl.BlockSpec((bsz, d), lambda i: (0, 0)),
            pl.BlockSpec((d, ADA_TN), lambda i: (0, i)),
            pl.BlockSpec((1, ADA_TN), lambda i: (0, i)),
        ],
        out_specs=pl.BlockSpec((bsz, ADA_TN), lambda i: (0, i)),
        compiler_params=pltpu.CompilerParams(
            dimension_semantics=("arbitrary",), vmem_limit_bytes=VMEM_LIMIT),
        name="ada",
    )(c, w, b.reshape(1, n))


def _norm_mod(x, g, shift, scale):
    ms = jnp.mean(x * x, axis=-1, keepdims=True)
    return (x * lax.rsqrt(ms + RMS_EPS)) * g * (1.0 + scale) + shift


def _inproj_kernel(x_ref, mod_ref, g_ref, w_ref, o_ref):
    h = _norm_mod(x_ref[...], g_ref[...], mod_ref[0, 0:1, :], mod_ref[0, 1:2, :])
    o_ref[...] = jnp.dot(h.astype(BF16), w_ref[...],
                         preferred_element_type=F32).astype(o_ref.dtype)


def _inproj(x2, mod3, g1, w_in, seq):
    t, d = x2.shape
    n = w_in.shape[1]
    per_b = seq // INPROJ_TM
    return pl.pallas_call(
        _inproj_kernel,
        out_shape=jax.ShapeDtypeStruct((t, n), BF16),
        grid=(t // INPROJ_TM,),
        in_specs=[
            pl.BlockSpec((INPROJ_TM, d), lambda i: (i, 0)),
            pl.BlockSpec((1, N_MOD, d), lambda i: (i // per_b, 0, 0)),
            pl.BlockSpec((1, d), lambda i: (0, 0)),
            pl.BlockSpec((d, n), lambda i: (0, 0)),
        ],
        out_specs=pl.BlockSpec((INPROJ_TM, n), lambda i: (i, 0)),
        compiler_params=pltpu.CompilerParams(
            dimension_semantics=("arbitrary",), vmem_limit_bytes=VMEM_LIMIT),
        name="inproj",
    )(x2, mod3, g1, w_in)


def _mix_kernel(sinks_ref, proj_ref, kvp_ref, up_ref, x_ref, mod_ref, wpool_ref,
                pscale_ref, wout_ref, g2_ref, wr_ref, br_ref,
                x1_ref, h2_ref, mi_ref, mf_ref, cnt_ref,
                ubuf, mixbuf, carry):
    b = pl.program_id(0)
    j = pl.program_id(1)
    tq = x_ref.shape[0]
    seq_start = j == 0

    @pl.when(jnp.logical_and(b == 0, j == 0))
    def _():
        carry[...] = jnp.zeros_like(carry)

    halo = up_ref[...].astype(F32)
    ubuf[0:MAX_POOL_WINDOW, :] = jnp.where(seq_start, 0.0, halo)
    ubuf[MAX_POOL_WINDOW:MAX_POOL_WINDOW + tq, :] = proj_ref[:, 0:D_POOL].astype(F32)
    pos = j * tq + lax.broadcasted_iota(I32, (tq, 1), 0)
    for gi, w in enumerate(POOL_WINDOWS):
        c0 = gi * POOL_GROUP
        u = ubuf[MAX_POOL_WINDOW:MAX_POOL_WINDOW + tq, c0:c0 + POOL_GROUP]
        acc = u
        for k in range(1, w):
            acc = acc + ubuf[MAX_POOL_WINDOW - k:MAX_POOL_WINDOW - k + tq, c0:c0 + POOL_GROUP]
        cnt = jnp.minimum(pos + 1, w).astype(F32)
        delta = (acc / cnt - u).astype(BF16)
        yp = jnp.dot(delta, wpool_ref[gi], preferred_element_type=F32)
        yp = yp * pscale_ref[:, c0:c0 + POOL_GROUP]
        mixbuf[:, c0:c0 + POOL_GROUP] = yp.astype(BF16)

    lane = lax.broadcasted_iota(I32, (2 * WINDOW, LANES), 1)
    low = lane < HEAD_DIM
    qi = lax.broadcasted_iota(I32, (WINDOW, 2 * WINDOW), 0)
    kj = lax.broadcasted_iota(I32, (WINDOW, 2 * WINDOW), 1)
    dist = qi - kj + WINDOW
    band = jnp.logical_and(dist >= 0, dist < WINDOW)
    olane = lax.broadcasted_iota(I32, (WINDOW, LANES), 1)
    k_col = D_POOL + D_ATTN
    v_col = k_col + D_KV
    for blk in range(tq // WINDOW):
        r0 = blk * WINDOW
        if blk == 0:
            kcat = jnp.concatenate(
                [kvp_ref[:, 0:D_KV], proj_ref[0:WINDOW, k_col:k_col + D_KV]], axis=0)
            vcat = jnp.concatenate(
                [kvp_ref[:, D_KV:2 * D_KV], proj_ref[0:WINDOW, v_col:v_col + D_KV]], axis=0)
            mask = jnp.logical_and(band, kj >= jnp.where(seq_start, WINDOW, 0))
        else:
            kcat = proj_ref[r0 - WINDOW:r0 + WINDOW, k_col:k_col + D_KV]
            vcat = proj_ref[r0 - WINDOW:r0 + WINDOW, v_col:v_col + D_KV]
            mask = band
        kswap = pltpu.roll(kcat.astype(F32), HEAD_DIM, 1).astype(BF16)
        vswap = pltpu.roll(vcat.astype(F32), HEAD_DIM, 1).astype(BF16)
        zero = jnp.zeros_like(kcat)
        for g in range(N_KV_HEADS):
            ksrc_lo, ksrc_hi = (kcat, kswap) if g == 0 else (kswap, kcat)
            vsrc_lo, vsrc_hi = (vcat, vswap) if g == 0 else (vswap, vcat)
            kbd = jnp.concatenate([jnp.where(low, ksrc_lo, zero),
                                   jnp.where(low, zero, ksrc_hi)], axis=0)
            vbd = jnp.concatenate([jnp.where(low, vsrc_lo, zero),
                                   jnp.where(low, zero, vsrc_hi)], axis=0)
            q_col = D_POOL + g * GQA_GROUP * HEAD_DIM
            q = jnp.concatenate(
                [proj_ref[r0:r0 + WINDOW, q_col + p * LANES:q_col + (p + 1) * LANES]
                 for p in range(PAIRS_PER_KV)], axis=0)
            q = q * jnp.asarray(HEAD_DIM ** -0.5, BF16)
            s = lax.dot_general(q, kbd, (((1,), (1,)), ((), ())),
                                preferred_element_type=F32)
            probs = []
            rdens = []
            for p in range(PAIRS_PER_KV):
                row_p = []
                row_r = []
                for hh in range(HEADS_PER_VREG):
                    sink = sinks_ref[g * GQA_GROUP + p * HEADS_PER_VREG + hh]
                    sp = s[p * WINDOW:(p + 1) * WINDOW,
                           hh * 2 * WINDOW:(hh + 1) * 2 * WINDOW]
                    sp = jnp.where(mask, sp, NEG_INF)
                    m = jnp.maximum(jnp.max(sp, axis=-1, keepdims=True), sink)
                    e = jnp.exp(sp - m)
                    den = jnp.sum(e, axis=-1, keepdims=True) + jnp.exp(sink - m)
                    row_p.append(e.astype(BF16))
                    row_r.append(1.0 / den)
                probs.append(jnp.concatenate(row_p, axis=1))
                rdens.append(row_r)
            pmat = jnp.concatenate(probs, axis=0)
            o = jnp.dot(pmat, vbd, preferred_element_type=F32)
            for p in range(PAIRS_PER_KV):
                op = o[p * WINDOW:(p + 1) * WINDOW, :]
                norm = jnp.where(olane < HEAD_DIM, rdens[p][0], rdens[p][1])
                c0 = D_POOL + g * GQA_GROUP * HEAD_DIM + p * LANES
                mixbuf[r0:r0 + WINDOW, c0:c0 + LANES] = (op * norm).astype(BF16)

    y = jnp.dot(mixbuf[...], wout_ref[...], preferred_element_type=F32)
    x1 = x_ref[...] + mod_ref[0, 2:3, :] * y
    x1_ref[...] = x1

    h2 = _norm_mod(x1, g2_ref[...], mod_ref[0, 3:4, :], mod_ref[0, 4:5, :])
    _store_slabs(h2_ref, h2)

    logits = jnp.dot(h2.astype(BF16), wr_ref[...], preferred_element_type=F32) + br_ref[...]
    ln = lax.broadcasted_iota(I32, (tq, ROUTER_LANES), 1)
    lnf = ln.astype(F32)
    ninf = -jnp.inf
    is_g = ln < N_EXPERT_GROUPS
    gl = jnp.where(is_g, logits, ninf)
    gmax = jnp.max(gl, axis=-1, keepdims=True)
    g_sel = jnp.min(jnp.where(gl == gmax, lnf, float(ROUTER_LANES)), axis=-1, keepdims=True)
    p_grp = 1.0 / jnp.sum(jnp.where(is_g, jnp.exp(logits - gmax), 0.0), axis=-1, keepdims=True)
    lane_grp = ((ln - N_EXPERT_GROUPS) >> 3).astype(F32)
    in_sel = jnp.logical_and(
        jnp.logical_and(ln >= N_EXPERT_GROUPS, ln < N_EXPERT_GROUPS + N_EXPERTS),
        lane_grp == g_sel)
    el = jnp.where(in_sel, logits, ninf)
    l1 = jnp.max(el, axis=-1, keepdims=True)
    i1 = jnp.min(jnp.where(el == l1, lnf, float(ROUTER_LANES)), axis=-1, keepdims=True)
    el2 = jnp.where(lnf == i1, ninf, el)
    l2 = jnp.max(el2, axis=-1, keepdims=True)
    i2 = jnp.min(jnp.where(el2 == l2, lnf, float(ROUTER_LANES)), axis=-1, keepdims=True)
    tt = jnp.exp(l2 - l1)
    w1 = 1.0 / (1.0 + tt)
    comb1 = p_grp * w1
    comb2 = p_grp * (tt * w1)
    hit1 = lnf == i1
    hit2 = lnf == i2
    onehot = jnp.where(jnp.logical_or(hit1, hit2), 1.0, 0.0)
    rr = lax.broadcasted_iota(I32, (tq, tq), 0)
    cc = lax.broadcasted_iota(I32, (tq, tq), 1)
    lower = jnp.where(rr > cc, 1.0, 0.0).astype(BF16)
    prior = jnp.dot(lower, onehot.astype(BF16), preferred_element_type=F32) + carry[...]
    rank1 = jnp.sum(jnp.where(hit1, prior, 0.0), axis=-1, keepdims=True)
    rank2 = jnp.sum(jnp.where(hit2, prior, 0.0), axis=-1, keepdims=True)
    carry[...] = carry[...] + jnp.sum(onehot, axis=0, keepdims=True)
    cnt_ref[...] = carry[...].astype(I32)
    eid1 = i1 - float(N_EXPERT_GROUPS)
    eid2 = i2 - float(N_EXPERT_GROUPS)
    meta = jnp.where(ln == 0, eid1, jnp.where(ln == 1, eid2,
                     jnp.where(ln == 2, rank1, jnp.where(ln == 3, rank2, 0.0))))
    mi_ref[...] = meta.T[0:META_ROWS, :].astype(I32)
    mf_ref[...] = jnp.where(ln == 0, comb1, jnp.where(ln == 1, comb2, 0.0))


def _mix(sinks, proj, x2, mod3, w_pool, pool_scale, w_out, g2, w_r, b_r, bsz, seq):
    t, d = x2.shape
    n_in = proj.shape[1]
    per_b = seq // MIX_TQ
    q_per_win = MIX_TQ // WINDOW
    q_per_halo = MIX_TQ // MAX_POOL_WINDOW
    slabs = d // (2 * LANES)

    def row(b, j):
        return b * per_b + j

    out_shapes = (
        jax.ShapeDtypeStruct((t, d), F32),
        jax.ShapeDtypeStruct((t * slabs, LANES), U32),
        jax.ShapeDtypeStruct((META_ROWS, t), I32),
        jax.ShapeDtypeStruct((t, ROUTER_LANES), F32),
        jax.ShapeDtypeStruct((1, ROUTER_LANES), I32),
    )
    return pl.pallas_call(
        _mix_kernel,
        out_shape=out_shapes,
        grid=(bsz, per_b),
        in_specs=[
            pl.BlockSpec(memory_space=pltpu.SMEM),
            pl.BlockSpec((MIX_TQ, n_in), lambda b, j: (row(b, j), 0)),
            pl.BlockSpec((WINDOW, 2 * D_KV),
                         lambda b, j: (jnp.maximum(row(b, j) * q_per_win - 1, 0),
                                       (D_POOL + D_ATTN) // (2 * D_KV))),
            pl.BlockSpec((MAX_POOL_WINDOW, D_POOL),
                         lambda b, j: (jnp.maximum(row(b, j) * q_per_halo - 1, 0), 0)),
            pl.BlockSpec((MIX_TQ, d), lambda b, j: (row(b, j), 0)),
            pl.BlockSpec((1, N_MOD, d), lambda b, j: (b, 0, 0)),
            pl.BlockSpec(w_pool.shape, lambda b, j: (0, 0, 0)),
            pl.BlockSpec((1, D_POOL), lambda b, j: (0, 0)),
            pl.BlockSpec((d, d), lambda b, j: (0, 0)),
            pl.BlockSpec((1, d), lambda b, j: (0, 0)),
            pl.BlockSpec((d, ROUTER_LANES), lambda b, j: (0, 0)),
            pl.BlockSpec((1, ROUTER_LANES), lambda b, j: (0, 0)),
        ],
        out_specs=(
            pl.BlockSpec((MIX_TQ, d), lambda b, j: (row(b, j), 0)),
            pl.BlockSpec((MIX_TQ * slabs, LANES), lambda b, j: (row(b, j), 0)),
            pl.BlockSpec((META_ROWS, MIX_TQ), lambda b, j: (0, row(b, j))),
            pl.BlockSpec((MIX_TQ, ROUTER_LANES), lambda b, j: (row(b, j), 0)),
            pl.BlockSpec((1, ROUTER_LANES), lambda b, j: (0, 0)),
        ),
        scratch_shapes=[
            pltpu.VMEM((MAX_POOL_WINDOW + MIX_TQ, D_POOL), F32),
            pltpu.VMEM((MIX_TQ, d), BF16),
            pltpu.VMEM((1, ROUTER_LANES), F32),
        ],
        compiler_params=pltpu.CompilerParams(
            dimension_semantics=("arbitrary", "arbitrary"), vmem_limit_bytes=VMEM_LIMIT),
        name="mix",
    )(sinks, proj, proj, proj, x2, mod3, w_pool, pool_scale, w_out, g2, w_r, b_r)


def _dispatch_kernel(pstart_ref, count_ref, e1_ref, e2_ref, r1_ref, r2_ref, h2_ref,
                     xr_hbm, d1_ref, d2_ref, zbuf, sem, zsem):
    tc = e1_ref.shape[0]
    step = pl.program_id(0)
    nblk = xr_hbm.shape[0] // MOE_BLK
    last = N_EXPERTS - 1
    n_used = (pstart_ref[last] + count_ref[last] + MOE_BLK - 1) // MOE_BLK

    def pad_copies(e):
        cnt = count_ref[e]
        npad = (-cnt) & (MOE_BLK - 1)
        off = pstart_ref[e] + cnt
        out = []
        for piece in PAD_PIECES:
            out.append((npad & piece, pltpu.make_async_copy(
                zbuf.at[pl.ds(0, piece)], xr_hbm.at[pl.ds(off, piece)], zsem)))
            off = off + (npad & piece)
        return out

    def tail_copy(blk):
        return pltpu.make_async_copy(zbuf, xr_hbm.at[pl.ds(blk * MOE_BLK, MOE_BLK)], zsem)

    def zero_fill(start):
        def per_expert(e, carry):
            for flag, cp in pad_copies(e):
                pl.when(flag != 0)(cp.start if start else cp.wait)
            return carry

        def per_tail(blk, carry):
            cp = tail_copy(blk)
            cp.start() if start else cp.wait()
            return carry

        lax.fori_loop(0, N_EXPERTS, per_expert, 0)
        lax.fori_loop(n_used, nblk, per_tail, 0)

    @pl.when(step == 0)
    def _():
        zbuf[...] = jnp.zeros_like(zbuf)
        zero_fill(True)

    def copies(tl, dst1, dst2):
        src = h2_ref.at[tl]
        return (pltpu.make_async_copy(src, xr_hbm.at[dst1], sem),
                pltpu.make_async_copy(src, xr_hbm.at[dst2], sem))

    def issue(i, carry):
        for u in range(DISPATCH_UNROLL):
            tl = i * DISPATCH_UNROLL + u
            dst1 = pstart_ref[e1_ref[tl]] + r1_ref[tl]
            dst2 = pstart_ref[e2_ref[tl]] + r2_ref[tl]
            d1_ref[tl] = dst1
            d2_ref[tl] = dst2
            for cp in copies(tl, dst1, dst2):
                cp.start()
        return carry

    def drain(i, carry):
        for u in range(DMA_UNROLL):
            tl = i * DMA_UNROLL + u
            for cp in copies(tl, d1_ref[tl], d2_ref[tl]):
                cp.wait()
        return carry

    lax.fori_loop(0, tc // DISPATCH_UNROLL, issue, 0)
    lax.fori_loop(0, tc // DMA_UNROLL, drain, 0)

    @pl.when(step == 0)
    def _():
        zero_fill(False)


def _dispatch(pstarts, counts, e1, e2, r1, r2, h2r, n_rows):
    t = e1.shape[0]
    tc = min(DISPATCH_TC, t)
    smem_blk = pl.BlockSpec((tc,), lambda i: (i,), memory_space=pltpu.SMEM)
    return pl.pallas_call(
        _dispatch_kernel,
        out_shape=(
            jax.ShapeDtypeStruct((n_rows,) + h2r.shape[1:], h2r.dtype),
            jax.ShapeDtypeStruct((t,), I32),
            jax.ShapeDtypeStruct((t,), I32),
        ),
        grid=(t // tc,),
        in_specs=[
            pl.BlockSpec(memory_space=pltpu.SMEM),
            pl.BlockSpec(memory_space=pltpu.SMEM),
            smem_blk, smem_blk, smem_blk, smem_blk,
            pl.BlockSpec((tc,) + h2r.shape[1:], lambda i: (i, 0, 0)),
        ],
        out_specs=(pl.BlockSpec(memory_space=pl.ANY), smem_blk, smem_blk),
        scratch_shapes=[pltpu.VMEM((MOE_BLK,) + h2r.shape[1:], h2r.dtype),
                        pltpu.SemaphoreType.DMA(()), pltpu.SemaphoreType.DMA(())],
        compiler_params=pltpu.CompilerParams(
            dimension_semantics=("arbitrary",), has_side_effects=True,
            vmem_limit_bytes=VMEM_LIMIT),
        name="dispatch",
    )(pstarts, counts, e1, e2, r1, r2, h2r)


def _experts_kernel(blk_e_ref, blk_row_ref, blk_valid_ref, blk_first_ref, blk_slot_ref,
                    blk_next_ref, xr_ref, wg_hbm, wu_hbm, wd_hbm, yr_ref,
                    wg_f32, wu_f32, wd_f32, wg_bf, wu_bf, wd_bf, xb, wsems):
    i = pl.program_id(0)
    nvalid = blk_valid_ref[i]
    slot = blk_slot_ref[i]

    def weight_copies(expert, s):
        return [pltpu.make_async_copy(src.at[expert], dst.at[s], wsems.at[s])
                for src, dst in ((wg_hbm, wg_f32), (wu_hbm, wu_f32), (wd_hbm, wd_f32))]

    @pl.when(i == 0)
    def _():
        for cp in weight_copies(blk_e_ref[0], slot):
            cp.start()

    @pl.when(blk_first_ref[i] == 1)
    def _():
        nxt = blk_next_ref[i]

        @pl.when(nxt >= 0)
        def _():
            for cp in weight_copies(nxt, 1 - slot):
                cp.start()

        for cp in weight_copies(blk_e_ref[i], slot):
            cp.wait()
        wg_bf[...] = wg_f32[slot].astype(BF16)
        wu_bf[...] = wu_f32[slot].astype(BF16)
        wd_bf[...] = wd_f32[slot].astype(BF16)

    @pl.when(nvalid > 0)
    def _():
        rows = xb.shape[0]
        for sidx, slab in enumerate(_load_slabs(xr_ref, rows)):
            xb[:, sidx * LANES:(sidx + 1) * LANES] = slab.astype(BF16)
        xv = xb[...]
        gate = jnp.dot(xv, wg_bf[...], preferred_element_type=F32)
        up = jnp.dot(xv, wu_bf[...], preferred_element_type=F32)
        act = (_silu(gate) * up).astype(BF16)
        _store_slabs(yr_ref, jnp.dot(act, wd_bf[...], preferred_element_type=F32))

    @pl.when(nvalid == 0)
    def _():
        _store_slabs(yr_ref, jnp.zeros((xb.shape[0], xb.shape[1]), F32))


def _experts(schedule, xr, w_gate, w_up, w_down):
    n_rows, slabs, _ = xr.shape
    d = 2 * slabs * LANES
    nblk = n_rows // MOE_BLK
    grid_spec = pltpu.PrefetchScalarGridSpec(
        num_scalar_prefetch=len(schedule),
        grid=(nblk,),
        in_specs=[
            pl.BlockSpec((MOE_BLK * slabs, LANES), lambda i, *sched: (sched[1][i], 0)),
            pl.BlockSpec(memory_space=pl.ANY),
            pl.BlockSpec(memory_space=pl.ANY),
            pl.BlockSpec(memory_space=pl.ANY),
        ],
        out_specs=pl.BlockSpec((MOE_BLK * slabs, LANES), lambda i, *sched: (i, 0)),
        scratch_shapes=[
            pltpu.VMEM((2, d, D_EXPERT), F32),
            pltpu.VMEM((2, d, D_EXPERT), F32),
            pltpu.VMEM((2, D_EXPERT, d), F32),
            pltpu.VMEM((d, D_EXPERT), BF16),
            pltpu.VMEM((d, D_EXPERT), BF16),
            pltpu.VMEM((D_EXPERT, d), BF16),
            pltpu.VMEM((MOE_BLK, d), BF16),
            pltpu.SemaphoreType.DMA((2,)),
        ],
    )
    yr = pl.pallas_call(
        _experts_kernel,
        out_shape=jax.ShapeDtypeStruct((n_rows * slabs, LANES), U32),
        grid_spec=grid_spec,
        compiler_params=pltpu.CompilerParams(
            dimension_semantics=("arbitrary",), vmem_limit_bytes=VMEM_LIMIT),
        name="experts",
    )(*schedule, xr.reshape(n_rows * slabs, LANES), w_gate, w_up, w_down)
    return yr.reshape(n_rows, slabs, LANES)


def _combine_kernel(d1_ref, d2_ref, yr_hbm, x1_ref, mf_ref, mod_ref, gf_ref, o_ref,
                    ya0, yb0, ya1, yb1, sems):
    i = pl.program_id(0)
    n = pl.num_programs(0)
    tk = x1_ref.shape[0]
    bufs = ((ya0, yb0), (ya1, yb1))

    slabs = yr_hbm.shape[1]

    def copies(tile, slot, tl):
        tok = tile * tk + tl
        ya, yb = bufs[slot]
        dst = pl.ds(pl.multiple_of(tl * slabs, slabs), slabs)
        return (pltpu.make_async_copy(yr_hbm.at[d1_ref[tok]], ya.at[dst], sems.at[slot]),
                pltpu.make_async_copy(yr_hbm.at[d2_ref[tok]], yb.at[dst], sems.at[slot]))

    def issue(tile, slot):
        def body(k, carry):
            for u in range(DMA_UNROLL):
                for cp in copies(tile, slot, k * DMA_UNROLL + u):
                    cp.start()
            return carry
        lax.fori_loop(0, tk // DMA_UNROLL, body, 0)

    def drain(tile, slot):
        def body(k, carry):
            for u in range(DMA_UNROLL):
                for cp in copies(tile, slot, k * DMA_UNROLL + u):
                    cp.wait()
            return carry
        lax.fori_loop(0, tk // DMA_UNROLL, body, 0)

    def step(slot):
        @pl.when(i == 0)
        def _():
            issue(i, slot)

        drain(i, slot)
        nxt = jnp.minimum(i + 1, n - 1)
        for tl in range(tk):
            for cp in copies(nxt, 1 - slot, tl):
                cp.start()
        ya, yb = bufs[slot]
        c1 = mf_ref[:, 0:1]
        c2 = mf_ref[:, 1:2]
        moe = jnp.concatenate(
            [c1 * sa + c2 * sb for sa, sb in zip(_load_slabs(ya, tk), _load_slabs(yb, tk))], axis=1)
        xo = x1_ref[...] + mod_ref[0, 5:6, :] * moe
        ms = jnp.mean(xo * xo, axis=-1, keepdims=True)
        o_ref[...] = (xo * lax.rsqrt(ms + RMS_EPS)) * gf_ref[...]

        @pl.when(i == n - 1)
        def _():
            drain(nxt, 1 - slot)

    @pl.when(i % 2 == 0)
    def _():
        step(0)

    @pl.when(i % 2 == 1)
    def _():
        step(1)


def _combine(d1, d2, yr, x1, mf, mod3, g_final, seq):
    t, d = x1.shape
    slabs = yr.shape[1]
    per_b = seq // COMBINE_TK
    grid_spec = pltpu.PrefetchScalarGridSpec(
        num_scalar_prefetch=2,
        grid=(t // COMBINE_TK,),
        in_specs=[
            pl.BlockSpec(memory_space=pl.ANY),
            pl.BlockSpec((COMBINE_TK, d), lambda i, a, b: (i, 0)),
            pl.BlockSpec((COMBINE_TK, ROUTER_LANES), lambda i, a, b: (i, 0)),
            pl.BlockSpec((1, N_MOD, d), lambda i, a, b: (i // per_b, 0, 0)),
            pl.BlockSpec((1, d), lambda i, a, b: (0, 0)),
        ],
        out_specs=pl.BlockSpec((COMBINE_TK, d), lambda i, a, b: (i, 0)),
        scratch_shapes=[pltpu.VMEM((COMBINE_TK * slabs, LANES), U32) for _ in range(4)]
        + [pltpu.SemaphoreType.DMA((2,))],
    )
    return pl.pallas_call(
        _combine_kernel,
        out_shape=jax.ShapeDtypeStruct((t, d), F32),
        grid_spec=grid_spec,
        compiler_params=pltpu.CompilerParams(
            dimension_semantics=("arbitrary",), vmem_limit_bytes=VMEM_LIMIT),
        name="combine",
    )(d1, d2, yr, x1, mf, mod3, g_final)


def _block_schedule(counts, n_rows):
    nblk = n_rows // MOE_BLK
    eidx = jnp.arange(N_EXPERTS, dtype=I32)
    bidx = jnp.arange(nblk, dtype=I32)
    nblocks = (counts + MOE_BLK - 1) // MOE_BLK
    bends = jnp.cumsum(nblocks)
    bstarts = bends - nblocks
    n_used = bends[-1]
    used = bidx < n_used
    blk_row = jnp.minimum(bidx, jnp.maximum(n_used - 1, 0))
    owner = jnp.logical_and(blk_row[:, None] >= bstarts[None, :], blk_row[:, None] < bends[None, :])

    def pick(per_expert):
        return jnp.sum(jnp.where(owner, per_expert[None, :], 0), axis=1).astype(I32)

    blk_e = pick(eidx)
    blk_valid = jnp.where(
        used, jnp.clip(pick(counts) - (bidx - pick(bstarts)) * MOE_BLK, 0, MOE_BLK), 0).astype(I32)
    blk_first = jnp.logical_and(used, bidx == pick(bstarts)).astype(I32)
    nonempty = counts > 0
    blk_slot = pick(jnp.cumsum(nonempty.astype(I32)) - 1) & 1
    later = jnp.logical_and(nonempty[None, :], eidx[None, :] > eidx[:, None])
    nxt = jnp.min(jnp.where(later, eidx[None, :], N_EXPERTS), axis=1)
    blk_next = pick(jnp.where(nxt < N_EXPERTS, nxt, -1))
    pstarts = (bstarts * MOE_BLK).astype(I32)
    return pstarts, (blk_e, blk_row, blk_valid, blk_first, blk_slot, blk_next)


def kernel(x, c, w_ada, b_ada, g_norm1, w_in, w_pool, pool_scale, attn_sinks, w_out,
           g_norm2, w_router_group, b_router_group, w_router_expert, b_router_expert,
           w_gate, w_up, w_down, g_final):
    bsz, seq, d = x.shape
    t = bsz * seq
    assert w_ada.shape[0] == 1, "single-layer model: the combine step applies the final norm"
    x2 = x.reshape(t, d)
    for l in range(1):
        mod3 = _ada(c, w_ada[l], b_ada[l]).reshape(bsz, N_MOD, d)
        proj = _inproj(x2, mod3, g_norm1[l].reshape(1, d), w_in[l].astype(BF16), seq)
        pad = ROUTER_LANES - N_EXPERT_GROUPS - N_EXPERTS
        w_r = jnp.concatenate(
            [w_router_group[l], w_router_expert[l], jnp.zeros((d, pad), F32)], axis=1).astype(BF16)
        b_r = jnp.concatenate(
            [b_router_group[l], b_router_expert[l], jnp.zeros((pad,), F32)]).reshape(1, ROUTER_LANES)
        x1, h2r, mi, mf, cnt = _mix(
            attn_sinks[l], proj, x2, mod3, w_pool[l].astype(BF16),
            pool_scale[l].reshape(1, D_POOL), w_out[l].astype(BF16),
            g_norm2[l].reshape(1, d), w_r, b_r, bsz, seq)
        n_rows = 2 * t + N_EXPERTS * MOE_BLK
        counts = cnt[0, N_EXPERT_GROUPS:N_EXPERT_GROUPS + N_EXPERTS]
        pstarts, schedule = _block_schedule(counts, n_rows)
        xr, d1, d2 = _dispatch(pstarts, counts, mi[0], mi[1], mi[2], mi[3],
                               h2r.reshape(t, d // (2 * LANES), LANES), n_rows)
        yr = _experts(schedule, xr, w_gate[l], w_up[l], w_down[l])
        x2 = _combine(d1, d2, yr, x1, mf, mod3, g_final.reshape(1, d), seq)
    return x2.reshape(bsz, seq, d)
```

```python
import functools

import jax
import jax.numpy as jnp
from jax import lax
from jax.experimental import pallas as pl
from jax.experimental.pallas import tpu as pltpu

F32 = jnp.float32
BF16 = jnp.bfloat16
I32 = jnp.int32
U32 = jnp.uint32

D_POOL = 1024
POOL_WINDOWS = (2, 4, 8, 16)
POOL_GROUP = 256
MAX_POOL_WINDOW = 16
HEAD_DIM = 64
N_KV_HEADS = 2
GQA_GROUP = 8
WINDOW = 128
D_ATTN = 1024
D_KV = 128
N_EXPERT_GROUPS = 8
EXPERTS_PER_GROUP = 8
N_EXPERTS = 64
D_EXPERT = 512
N_MOD = 6
RMS_EPS = 1e-6
NEG_INF = -1e30

LANES = 128
HEADS_PER_VREG = LANES // HEAD_DIM
PAIRS_PER_KV = GQA_GROUP // HEADS_PER_VREG

ADA_TN = 1024
INPROJ_TM = 512
MIX_TQ = 256
MOE_BLK = 256
PAD_PIECES = tuple(MOE_BLK >> (k + 1) for k in range(MOE_BLK.bit_length() - 1))
DISPATCH_TC = 1024
COMBINE_TK = 256
DMA_UNROLL = 8
DMA_THREADS = 2
DISPATCH_UNROLL = 64
ROUTER_LANES = 128
META_ROWS = 8
VMEM_LIMIT = 56 * 1024 * 1024


def _silu(v):
    return v * jax.nn.sigmoid(v)


def _load_slabs(ref, rows):
    slabs = ref.shape[0] // rows
    words = [ref[pl.ds(s, rows, stride=slabs), :] for s in range(slabs)]
    return [pltpu.unpack_elementwise(w, index=half, packed_dtype=BF16, unpacked_dtype=F32)
            for half in range(2) for w in words]


def _store_slabs(ref, val):
    rows, d = val.shape
    slabs = ref.shape[0] // rows
    for s in range(slabs):
        lo = val[:, s * LANES:(s + 1) * LANES]
        hi = val[:, d // 2 + s * LANES:d // 2 + (s + 1) * LANES]
        ref[pl.ds(s, rows, stride=slabs), :] = pltpu.pack_elementwise([lo, hi], packed_dtype=BF16)


def _ada_kernel(c_ref, w_ref, b_ref, o_ref):
    ca = _silu(c_ref[...])
    o_ref[...] = jnp.dot(ca.astype(BF16), w_ref[...].astype(BF16),
                         preferred_element_type=F32) + b_ref[...]


def _ada(c, w, b):
    bsz, d = c.shape
    n = w.shape[1]
    return pl.pallas_call(
        _ada_kernel,
        out_shape=jax.ShapeDtypeStruct((bsz, n), F32),
        grid=(n // ADA_TN,),
        in_specs=[
            pl.BlockSpec((bsz, d), lambda i: (0, 0)),
            pl.BlockSpec((d, ADA_TN), lambda i: (0, i)),
            pl.BlockSpec((1, ADA_TN), lambda i: (0, i)),
        ],
        out_specs=pl.BlockSpec((bsz, ADA_TN), lambda i: (0, i)),
        compiler_params=pltpu.CompilerParams(
            dimension_semantics=("arbitrary",), vmem_limit_bytes=VMEM_LIMIT),
        name="ada",
    )(c, w, b.reshape(1, n))


def _norm_mod(x, g, shift, scale):
    ms = jnp.mean(x * x, axis=-1, keepdims=True)
    return (x * lax.rsqrt(ms + RMS_EPS)) * g * (1.0 + scale) + shift


def _inproj_kernel(x_ref, mod_ref, g_ref, w_ref, o_ref):
    h = _norm_mod(x_ref[...], g_ref[...], mod_ref[0, 0:1, :], mod_ref[0, 1:2, :])
    o_ref[...] = jnp.dot(h.astype(BF16), w_ref[...],
                         preferred_element_type=F32).astype(o_ref.dtype)


def _inproj(x2, mod3, g1, w_in, seq):
    t, d = x2.shape
    n = w_in.shape[1]
    per_b = seq // INPROJ_TM
    return pl.pallas_call(
        _inproj_kernel,
        out_shape=jax.ShapeDtypeStruct((t, n), BF16),
        grid=(t // INPROJ_TM,),
        in_specs=[
            pl.BlockSpec((INPROJ_TM, d), lambda i: (i, 0)),
            pl.BlockSpec((1, N_MOD, d), lambda i: (i // per_b, 0, 0)),
            pl.BlockSpec((1, d), lambda i: (0, 0)),
            pl.BlockSpec((d, n), lambda i: (0, 0)),
        ],
        out_specs=pl.BlockSpec((INPROJ_TM, n), lambda i: (i, 0)),
        compiler_params=pltpu.CompilerParams(
            dimension_semantics=("arbitrary",), vmem_limit_bytes=VMEM_LIMIT),
        name="inproj",
    )(x2, mod3, g1, w_in)


def _mix_kernel(sinks_ref, proj_ref, kvp_ref, up_ref, x_ref, mod_ref, wpool_ref,
                pscale_ref, wout_ref, g2_ref, wr_ref, br_ref,
                x1_ref, h2_ref, mi_ref, mf_ref, cnt_ref,
                ubuf, mixbuf, carry):
    b = pl.program_id(0)
    j = pl.program_id(1)
    tq = x_ref.shape[0]
    seq_start = j == 0

    @pl.when(jnp.logical_and(b == 0, j == 0))
    def _():
        carry[...] = jnp.zeros_like(carry)

    halo = up_ref[...].astype(F32)
    ubuf[0:MAX_POOL_WINDOW, :] = jnp.where(seq_start, 0.0, halo)
    ubuf[MAX_POOL_WINDOW:MAX_POOL_WINDOW + tq, :] = proj_ref[:, 0:D_POOL].astype(F32)
    pos = j * tq + lax.broadcasted_iota(I32, (tq, 1), 0)
    for gi, w in enumerate(POOL_WINDOWS):
        c0 = gi * POOL_GROUP
        u = ubuf[MAX_POOL_WINDOW:MAX_POOL_WINDOW + tq, c0:c0 + POOL_GROUP]
        acc = u
        for k in range(1, w):
            acc = acc + ubuf[MAX_POOL_WINDOW - k:MAX_POOL_WINDOW - k + tq, c0:c0 + POOL_GROUP]
        cnt = jnp.minimum(pos + 1, w).astype(F32)
        delta = (acc / cnt - u).astype(BF16)
        yp = jnp.dot(delta, wpool_ref[gi], preferred_element_type=F32)
        yp = yp * pscale_ref[:, c0:c0 + POOL_GROUP]
        mixbuf[:, c0:c0 + POOL_GROUP] = yp.astype(BF16)

    lane = lax.broadcasted_iota(I32, (2 * WINDOW, LANES), 1)
    low = lane < HEAD_DIM
    qi = lax.broadcasted_iota(I32, (WINDOW, 2 * WINDOW), 0)
    kj = lax.broadcasted_iota(I32, (WINDOW, 2 * WINDOW), 1)
    dist = qi - kj + WINDOW
    band = jnp.logical_and(dist >= 0, dist < WINDOW)
    olane = lax.broadcasted_iota(I32, (WINDOW, LANES), 1)
    k_col = D_POOL + D_ATTN
    v_col = k_col + D_KV
    for blk in range(tq // WINDOW):
        r0 = blk * WINDOW
        if blk == 0:
            kcat = jnp.concatenate(
                [kvp_ref[:, 0:D_KV], proj_ref[0:WINDOW, k_col:k_col + D_KV]], axis=0)
            vcat = jnp.concatenate(
                [kvp_ref[:, D_KV:2 * D_KV], proj_ref[0:WINDOW, v_col:v_col + D_KV]], axis=0)
            mask = jnp.logical_and(band, kj >= jnp.where(seq_start, WINDOW, 0))
        else:
            kcat = proj_ref[r0 - WINDOW:r0 + WINDOW, k_col:k_col + D_KV]
            vcat = proj_ref[r0 - WINDOW:r0 + WINDOW, v_col:v_col + D_KV]
            mask = band
        kswap = pltpu.roll(kcat.astype(F32), HEAD_DIM, 1).astype(BF16)
        vswap = pltpu.roll(vcat.astype(F32), HEAD_DIM, 1).astype(BF16)
        zero = jnp.zeros_like(kcat)
        for g in range(N_KV_HEADS):
            ksrc_lo, ksrc_hi = (kcat, kswap) if g == 0 else (kswap, kcat)
            vsrc_lo, vsrc_hi = (vcat, vswap) if g == 0 else (vswap, vcat)
            kbd = jnp.concatenate([jnp.where(low, ksrc_lo, zero),
                                   jnp.where(low, zero, ksrc_hi)], axis=0)
            vbd = jnp.concatenate([jnp.where(low, vsrc_lo, zero),
                                   jnp.where(low, zero, vsrc_hi)], axis=0)
            q_col = D_POOL + g * GQA_GROUP * HEAD_DIM
            q = jnp.concatenate(
                [proj_ref[r0:r0 + WINDOW, q_col + p * LANES:q_col + (p + 1) * LANES]
                 for p in range(PAIRS_PER_KV)], axis=0)
            q = q * jnp.asarray(HEAD_DIM ** -0.5, BF16)
            s = lax.dot_general(q, kbd, (((1,), (1,)), ((), ())),
                                preferred_element_type=F32)
            probs = []
            rdens = []
            for p in range(PAIRS_PER_KV):
                row_p = []
                row_r = []
                for hh in range(HEADS_PER_VREG):
                    sink = sinks_ref[g * GQA_GROUP + p * HEADS_PER_VREG + hh]
                    sp = s[p * WINDOW:(p + 1) * WINDOW,
                           hh * 2 * WINDOW:(hh + 1) * 2 * WINDOW]
                    sp = jnp.where(mask, sp, NEG_INF)
                    m = jnp.maximum(jnp.max(sp, axis=-1, keepdims=True), sink)
                    e = jnp.exp(sp - m)
                    den = jnp.sum(e, axis=-1, keepdims=True) + jnp.exp(sink - m)
                    row_p.append(e.astype(BF16))
                    row_r.append(1.0 / den)
                probs.append(jnp.concatenate(row_p, axis=1))
                rdens.append(row_r)
            pmat = jnp.concatenate(probs, axis=0)
            o = jnp.dot(pmat, vbd, preferred_element_type=F32)
            for p in range(PAIRS_PER_KV):
                op = o[p * WINDOW:(p + 1) * WINDOW, :]
                norm = jnp.where(olane < HEAD_DIM, rdens[p][0], rdens[p][1])
                c0 = D_POOL + g * GQA_GROUP * HEAD_DIM + p * LANES
                mixbuf[r0:r0 + WINDOW, c0:c0 + LANES] = (op * norm).astype(BF16)

    y = jnp.dot(mixbuf[...], wout_ref[...], preferred_element_type=F32)
    x1 = x_ref[...] + mod_ref[0, 2:3, :] * y
    x1_ref[...] = x1

    h2 = _norm_mod(x1, g2_ref[...], mod_ref[0, 3:4, :], mod_ref[0, 4:5, :])
    _store_slabs(h2_ref, h2)

    logits = jnp.dot(h2.astype(BF16), wr_ref[...], preferred_element_type=F32) + br_ref[...]
    ln = lax.broadcasted_iota(I32, (tq, ROUTER_LANES), 1)
    lnf = ln.astype(F32)
    ninf = -jnp.inf
    is_g = ln < N_EXPERT_GROUPS
    gl = jnp.where(is_g, logits, ninf)
    gmax = jnp.max(gl, axis=-1, keepdims=True)
    g_sel = jnp.min(jnp.where(gl == gmax, lnf, float(ROUTER_LANES)), axis=-1, keepdims=True)
    p_grp = 1.0 / jnp.sum(jnp.where(is_g, jnp.exp(logits - gmax), 0.0), axis=-1, keepdims=True)
    lane_grp = ((ln - N_EXPERT_GROUPS) >> 3).astype(F32)
    in_sel = jnp.logical_and(
        jnp.logical_and(ln >= N_EXPERT_GROUPS, ln < N_EXPERT_GROUPS + N_EXPERTS),
        lane_grp == g_sel)
    el = jnp.where(in_sel, logits, ninf)
    l1 = jnp.max(el, axis=-1, keepdims=True)
    i1 = jnp.min(jnp.where(el == l1, lnf, float(ROUTER_LANES)), axis=-1, keepdims=True)
    el2 = jnp.where(lnf == i1, ninf, el)
    l2 = jnp.max(el2, axis=-1, keepdims=True)
    i2 = jnp.min(jnp.where(el2 == l2, lnf, float(ROUTER_LANES)), axis=-1, keepdims=True)
    tt = jnp.exp(l2 - l1)
    w1 = 1.0 / (1.0 + tt)
    comb1 = p_grp * w1
    comb2 = p_grp * (tt * w1)
    hit1 = lnf == i1
    hit2 = lnf == i2
    onehot = jnp.where(jnp.logical_or(hit1, hit2), 1.0, 0.0)
    rr = lax.broadcasted_iota(I32, (tq, tq), 0)
    cc = lax.broadcasted_iota(I32, (tq, tq), 1)
    lower = jnp.where(rr > cc, 1.0, 0.0).astype(BF16)
    prior = jnp.dot(lower, onehot.astype(BF16), preferred_element_type=F32) + carry[...]
    rank1 = jnp.sum(jnp.where(hit1, prior, 0.0), axis=-1, keepdims=True)
    rank2 = jnp.sum(jnp.where(hit2, prior, 0.0), axis=-1, keepdims=True)
    carry[...] = carry[...] + jnp.sum(onehot, axis=0, keepdims=True)
    cnt_ref[...] = carry[...].astype(I32)
    eid1 = i1 - float(N_EXPERT_GROUPS)
    eid2 = i2 - float(N_EXPERT_GROUPS)
    meta = jnp.where(ln == 0, eid1, jnp.where(ln == 1, eid2,
                     jnp.where(ln == 2, rank1, jnp.where(ln == 3, rank2, 0.0))))
    mi_ref[...] = meta.T[0:META_ROWS, :].astype(I32)
    mf_ref[...] = jnp.where(ln == 0, comb1, jnp.where(ln == 1, comb2, 0.0))


def _mix(sinks, proj, x2, mod3, w_pool, pool_scale, w_out, g2, w_r, b_r, bsz, seq):
    t, d = x2.shape
    n_in = proj.shape[1]
    per_b = seq // MIX_TQ
    q_per_win = MIX_TQ // WINDOW
    q_per_halo = MIX_TQ // MAX_POOL_WINDOW
    slabs = d // (2 * LANES)

    def row(b, j):
        return b * per_b + j

    out_shapes = (
        jax.ShapeDtypeStruct((t, d), F32),
        jax.ShapeDtypeStruct((t * slabs, LANES), U32),
        jax.ShapeDtypeStruct((META_ROWS, t), I32),
        jax.ShapeDtypeStruct((t, ROUTER_LANES), F32),
        jax.ShapeDtypeStruct((1, ROUTER_LANES), I32),
    )
    return pl.pallas_call(
        _mix_kernel,
        out_shape=out_shapes,
        grid=(bsz, per_b),
        in_specs=[
            pl.BlockSpec(memory_space=pltpu.SMEM),
            pl.BlockSpec((MIX_TQ, n_in), lambda b, j: (row(b, j), 0)),
            pl.BlockSpec((WINDOW, 2 * D_KV),
                         lambda b, j: (jnp.maximum(row(b, j) * q_per_win - 1, 0),
                                       (D_POOL + D_ATTN) // (2 * D_KV))),
            pl.BlockSpec((MAX_POOL_WINDOW, D_POOL),
                         lambda b, j: (jnp.maximum(row(b, j) * q_per_halo - 1, 0), 0)),
            pl.BlockSpec((MIX_TQ, d), lambda b, j: (row(b, j), 0)),
            pl.BlockSpec((1, N_MOD, d), lambda b, j: (b, 0, 0)),
            pl.BlockSpec(w_pool.shape, lambda b, j: (0, 0, 0)),
            pl.BlockSpec((1, D_POOL), lambda b, j: (0, 0)),
            pl.BlockSpec((d, d), lambda b, j: (0, 0)),
            pl.BlockSpec((1, d), lambda b, j: (0, 0)),
            pl.BlockSpec((d, ROUTER_LANES), lambda b, j: (0, 0)),
            pl.BlockSpec((1, ROUTER_LANES), lambda b, j: (0, 0)),
        ],
        out_specs=(
            pl.BlockSpec((MIX_TQ, d), lambda b, j: (row(b, j), 0)),
            pl.BlockSpec((MIX_TQ * slabs, LANES), lambda b, j: (row(b, j), 0)),
            pl.BlockSpec((META_ROWS, MIX_TQ), lambda b, j: (0, row(b, j))),
            pl.BlockSpec((MIX_TQ, ROUTER_LANES), lambda b, j: (row(b, j), 0)),
            pl.BlockSpec((1, ROUTER_LANES), lambda b, j: (0, 0)),
        ),
        scratch_shapes=[
            pltpu.VMEM((MAX_POOL_WINDOW + MIX_TQ, D_POOL), F32),
            pltpu.VMEM((MIX_TQ, d), BF16),
            pltpu.VMEM((1, ROUTER_LANES), F32),
        ],
        compiler_params=pltpu.CompilerParams(
            dimension_semantics=("arbitrary", "arbitrary"), vmem_limit_bytes=VMEM_LIMIT),
        name="mix",
    )(sinks, proj, proj, proj, x2, mod3, w_pool, pool_scale, w_out, g2, w_r, b_r)


def _dispatch_kernel(pstart_ref, count_ref, e1_ref, e2_ref, r1_ref, r2_ref, h2_ref,
                     xr_hbm, d1_ref, d2_ref, zbuf, sem, zsem):
    tc = e1_ref.shape[0]
    step = pl.program_id(0)
    nblk = xr_hbm.shape[0] // MOE_BLK
    last = N_EXPERTS - 1
    n_used = (pstart_ref[last] + count_ref[last] + MOE_BLK - 1) // MOE_BLK

    def pad_copies(e):
        cnt = count_ref[e]
        npad = (-cnt) & (MOE_BLK - 1)
        off = pstart_ref[e] + cnt
        out = []
        for piece in PAD_PIECES:
            out.append((npad & piece, pltpu.make_async_copy(
                zbuf.at[pl.ds(0, piece)], xr_hbm.at[pl.ds(off, piece)], zsem)))
            off = off + (npad & piece)
        return out

    def tail_copy(blk):
        return pltpu.make_async_copy(zbuf, xr_hbm.at[pl.ds(blk * MOE_BLK, MOE_BLK)], zsem)

    def zero_fill(start):
        def per_expert(e, carry):
            for flag, cp in pad_copies(e):
                pl.when(flag != 0)(cp.start if start else cp.wait)
            return carry

        def per_tail(blk, carry):
            cp = tail_copy(blk)
            cp.start() if start else cp.wait()
            return carry

        lax.fori_loop(0, N_EXPERTS, per_expert, 0)
        lax.fori_loop(n_used, nblk, per_tail, 0)

    @pl.when(step == 0)
    def _():
        zbuf[...] = jnp.zeros_like(zbuf)
        zero_fill(True)

    def copies(tl, dst1, dst2):
        src = h2_ref.at[tl]
        return (pltpu.make_async_copy(src, xr_hbm.at[dst1], sem),
                pltpu.make_async_copy(src, xr_hbm.at[dst2], sem))

    def issue(i, carry):
        for u in range(DISPATCH_UNROLL):
            tl = i * DISPATCH_UNROLL + u
            dst1 = pstart_ref[e1_ref[tl]] + r1_ref[tl]
            dst2 = pstart_ref[e2_ref[tl]] + r2_ref[tl]
            d1_ref[tl] = dst1
            d2_ref[tl] = dst2
            for k, cp in enumerate(copies(tl, dst1, dst2)):
                cp.start(priority=(2 * u + k) % DMA_THREADS)
        return carry

    def drain(i, carry):
        for u in range(DMA_UNROLL):
            tl = i * DMA_UNROLL + u
            for cp in copies(tl, d1_ref[tl], d2_ref[tl]):
                cp.wait()
        return carry

    lax.fori_loop(0, tc // DISPATCH_UNROLL, issue, 0)
    lax.fori_loop(0, tc // DMA_UNROLL, drain, 0)

    @pl.when(step == 0)
    def _():
        zero_fill(False)


def _dispatch(pstarts, counts, e1, e2, r1, r2, h2r, n_rows):
    t = e1.shape[0]
    tc = min(DISPATCH_TC, t)
    smem_blk = pl.BlockSpec((tc,), lambda i: (i,), memory_space=pltpu.SMEM)
    return pl.pallas_call(
        _dispatch_kernel,
        out_shape=(
            jax.ShapeDtypeStruct((n_rows,) + h2r.shape[1:], h2r.dtype),
            jax.ShapeDtypeStruct((t,), I32),
            jax.ShapeDtypeStruct((t,), I32),
        ),
        grid=(t // tc,),
        in_specs=[
            pl.BlockSpec(memory_space=pltpu.SMEM),
            pl.BlockSpec(memory_space=pltpu.SMEM),
            smem_blk, smem_blk, smem_blk, smem_blk,
            pl.BlockSpec((tc,) + h2r.shape[1:], lambda i: (i, 0, 0)),
        ],
        out_specs=(pl.BlockSpec(memory_space=pl.ANY), smem_blk, smem_blk),
        scratch_shapes=[pltpu.VMEM((MOE_BLK,) + h2r.shape[1:], h2r.dtype),
                        pltpu.SemaphoreType.DMA(()), pltpu.SemaphoreType.DMA(())],
        compiler_params=pltpu.CompilerParams(
            dimension_semantics=("arbitrary",), has_side_effects=True,
            vmem_limit_bytes=VMEM_LIMIT),
        name="dispatch",
    )(pstarts, counts, e1, e2, r1, r2, h2r)


def _experts_kernel(blk_e_ref, blk_row_ref, blk_valid_ref, blk_first_ref, blk_slot_ref,
                    blk_next_ref, xr_ref, wg_hbm, wu_hbm, wd_hbm, yr_ref,
                    wg_f32, wu_f32, wd_f32, wg_bf, wu_bf, wd_bf, xb, wsems):
    i = pl.program_id(0)
    nvalid = blk_valid_ref[i]
    slot = blk_slot_ref[i]

    def weight_copies(expert, s):
        return [pltpu.make_async_copy(src.at[expert], dst.at[s], wsems.at[s])
                for src, dst in ((wg_hbm, wg_f32), (wu_hbm, wu_f32), (wd_hbm, wd_f32))]

    @pl.when(i == 0)
    def _():
        for cp in weight_copies(blk_e_ref[0], slot):
            cp.start()

    @pl.when(blk_first_ref[i] == 1)
    def _():
        nxt = blk_next_ref[i]

        @pl.when(nxt >= 0)
        def _():
            for cp in weight_copies(nxt, 1 - slot):
                cp.start()

        for cp in weight_copies(blk_e_ref[i], slot):
            cp.wait()
        wg_bf[...] = wg_f32[slot].astype(BF16)
        wu_bf[...] = wu_f32[slot].astype(BF16)
        wd_bf[...] = wd_f32[slot].astype(BF16)

    @pl.when(nvalid > 0)
    def _():
        rows = xb.shape[0]
        for sidx, slab in enumerate(_load_slabs(xr_ref, rows)):
            xb[:, sidx * LANES:(sidx + 1) * LANES] = slab.astype(BF16)
        xv = xb[...]
        gate = jnp.dot(xv, wg_bf[...], preferred_element_type=F32)
        up = jnp.dot(xv, wu_bf[...], preferred_element_type=F32)
        act = (_silu(gate) * up).astype(BF16)
        _store_slabs(yr_ref, jnp.dot(act, wd_bf[...], preferred_element_type=F32))

    @pl.when(nvalid == 0)
    def _():
        _store_slabs(yr_ref, jnp.zeros((xb.shape[0], xb.shape[1]), F32))


def _experts(schedule, xr, w_gate, w_up, w_down):
    n_rows, slabs, _ = xr.shape
    d = 2 * slabs * LANES
    nblk = n_rows // MOE_BLK
    grid_spec = pltpu.PrefetchScalarGridSpec(
        num_scalar_prefetch=len(schedule),
        grid=(nblk,),
        in_specs=[
            pl.BlockSpec((MOE_BLK * slabs, LANES), lambda i, *sched: (sched[1][i], 0)),
            pl.BlockSpec(memory_space=pl.ANY),
            pl.BlockSpec(memory_space=pl.ANY),
            pl.BlockSpec(memory_space=pl.ANY),
        ],
        out_specs=pl.BlockSpec((MOE_BLK * slabs, LANES), lambda i, *sched: (i, 0)),
        scratch_shapes=[
            pltpu.VMEM((2, d, D_EXPERT), F32),
            pltpu.VMEM((2, d, D_EXPERT), F32),
            pltpu.VMEM((2, D_EXPERT, d), F32),
            pltpu.VMEM((d, D_EXPERT), BF16),
            pltpu.VMEM((d, D_EXPERT), BF16),
            pltpu.VMEM((D_EXPERT, d), BF16),
            pltpu.VMEM((MOE_BLK, d), BF16),
            pltpu.SemaphoreType.DMA((2,)),
        ],
    )
    yr = pl.pallas_call(
        _experts_kernel,
        out_shape=jax.ShapeDtypeStruct((n_rows * slabs, LANES), U32),
        grid_spec=grid_spec,
        compiler_params=pltpu.CompilerParams(
            dimension_semantics=("arbitrary",), vmem_limit_bytes=VMEM_LIMIT),
        name="experts",
    )(*schedule, xr.reshape(n_rows * slabs, LANES), w_gate, w_up, w_down)
    return yr.reshape(n_rows, slabs, LANES)


def _combine_kernel(d1_ref, d2_ref, yr_hbm, x1_ref, mf_ref, mod_ref, gf_ref, o_ref,
                    ya0, yb0, ya1, yb1, sems):
    i = pl.program_id(0)
    n = pl.num_programs(0)
    tk = x1_ref.shape[0]
    bufs = ((ya0, yb0), (ya1, yb1))

    slabs = yr_hbm.shape[1]

    def copies(tile, slot, tl):
        tok = tile * tk + tl
        ya, yb = bufs[slot]
        dst = pl.ds(pl.multiple_of(tl * slabs, slabs), slabs)
        return (pltpu.make_async_copy(yr_hbm.at[d1_ref[tok]], ya.at[dst], sems.at[slot]),
                pltpu.make_async_copy(yr_hbm.at[d2_ref[tok]], yb.at[dst], sems.at[slot]))

    def issue(tile, slot):
        def body(k, carry):
            for u in range(DMA_UNROLL):
                for c, cp in enumerate(copies(tile, slot, k * DMA_UNROLL + u)):
                    cp.start(priority=(2 * u + c) % DMA_THREADS)
            return carry
        lax.fori_loop(0, tk // DMA_UNROLL, body, 0)

    def drain(tile, slot):
        def body(k, carry):
            for u in range(DMA_UNROLL):
                for cp in copies(tile, slot, k * DMA_UNROLL + u):
                    cp.wait()
            return carry
        lax.fori_loop(0, tk // DMA_UNROLL, body, 0)

    def step(slot):
        @pl.when(i == 0)
        def _():
            issue(i, slot)

        drain(i, slot)
        nxt = jnp.minimum(i + 1, n - 1)
        for tl in range(tk):
            for c, cp in enumerate(copies(nxt, 1 - slot, tl)):
                cp.start(priority=(2 * tl + c) % DMA_THREADS)
        ya, yb = bufs[slot]
        c1 = mf_ref[:, 0:1]
        c2 = mf_ref[:, 1:2]
        moe = jnp.concatenate(
            [c1 * sa + c2 * sb for sa, sb in zip(_load_slabs(ya, tk), _load_slabs(yb, tk))], axis=1)
        xo = x1_ref[...] + mod_ref[0, 5:6, :] * moe
        ms = jnp.mean(xo * xo, axis=-1, keepdims=True)
        o_ref[...] = (xo * lax.rsqrt(ms + RMS_EPS)) * gf_ref[...]

        @pl.when(i == n - 1)
        def _():
            drain(nxt, 1 - slot)

    @pl.when(i % 2 == 0)
    def _():
        step(0)

    @pl.when(i % 2 == 1)
    def _():
        step(1)


def _combine(d1, d2, yr, x1, mf, mod3, g_final, seq):
    t, d = x1.shape
    slabs = yr.shape[1]
    per_b = seq // COMBINE_TK
    grid_spec = pltpu.PrefetchScalarGridSpec(
        num_scalar_prefetch=2,
        grid=(t // COMBINE_TK,),
        in_specs=[
            pl.BlockSpec(memory_space=pl.ANY),
            pl.BlockSpec((COMBINE_TK, d), lambda i, a, b: (i, 0)),
            pl.BlockSpec((COMBINE_TK, ROUTER_LANES), lambda i, a, b: (i, 0)),
            pl.BlockSpec((1, N_MOD, d), lambda i, a, b: (i // per_b, 0, 0)),
            pl.BlockSpec((1, d), lambda i, a, b: (0, 0)),
        ],
        out_specs=pl.BlockSpec((COMBINE_TK, d), lambda i, a, b: (i, 0)),
        scratch_shapes=[pltpu.VMEM((COMBINE_TK * slabs, LANES), U32) for _ in range(4)]
        + [pltpu.SemaphoreType.DMA((2,))],
    )
    return pl.pallas_call(
        _combine_kernel,
        out_shape=jax.ShapeDtypeStruct((t, d), F32),
        grid_spec=grid_spec,
        compiler_params=pltpu.CompilerParams(
            dimension_semantics=("arbitrary",), vmem_limit_bytes=VMEM_LIMIT),
        name="combine",
    )(d1, d2, yr, x1, mf, mod3, g_final)


def _block_schedule(counts, n_rows):
    nblk = n_rows // MOE_BLK
    eidx = jnp.arange(N_EXPERTS, dtype=I32)
    bidx = jnp.arange(nblk, dtype=I32)
    nblocks = (counts + MOE_BLK - 1) // MOE_BLK
    bends = jnp.cumsum(nblocks)
    bstarts = bends - nblocks
    n_used = bends[-1]
    used = bidx < n_used
    blk_row = jnp.minimum(bidx, jnp.maximum(n_used - 1, 0))
    owner = jnp.logical_and(blk_row[:, None] >= bstarts[None, :], blk_row[:, None] < bends[None, :])

    def pick(per_expert):
        return jnp.sum(jnp.where(owner, per_expert[None, :], 0), axis=1).astype(I32)

    blk_e = pick(eidx)
    blk_valid = jnp.where(
        used, jnp.clip(pick(counts) - (bidx - pick(bstarts)) * MOE_BLK, 0, MOE_BLK), 0).astype(I32)
    blk_first = jnp.logical_and(used, bidx == pick(bstarts)).astype(I32)
    nonempty = counts > 0
    blk_slot = pick(jnp.cumsum(nonempty.astype(I32)) - 1) & 1
    later = jnp.logical_and(nonempty[None, :], eidx[None, :] > eidx[:, None])
    nxt = jnp.min(jnp.where(later, eidx[None, :], N_EXPERTS), axis=1)
    blk_next = pick(jnp.where(nxt < N_EXPERTS, nxt, -1))
    pstarts = (bstarts * MOE_BLK).astype(I32)
    return pstarts, (blk_e, blk_row, blk_valid, blk_first, blk_slot, blk_next)


def kernel(x, c, w_ada, b_ada, g_norm1, w_in, w_pool, pool_scale, attn_sinks, w_out,
           g_norm2, w_router_group, b_router_group, w_router_expert, b_router_expert,
           w_gate, w_up, w_down, g_final):
    bsz, seq, d = x.shape
    t = bsz * seq
    assert w_ada.shape[0] == 1, "single-layer model: the combine step applies the final norm"
    x2 = x.reshape(t, d)
    for l in range(1):
        mod3 = _ada(c, w_ada[l], b_ada[l]).reshape(bsz, N_MOD, d)
        proj = _inproj(x2, mod3, g_norm1[l].reshape(1, d), w_in[l].astype(BF16), seq)
        pad = ROUTER_LANES - N_EXPERT_GROUPS - N_EXPERTS
        w_r = jnp.concatenate(
            [w_router_group[l], w_router_expert[l], jnp.zeros((d, pad), F32)], axis=1).astype(BF16)
        b_r = jnp.concatenate(
            [b_router_group[l], b_router_expert[l], jnp.zeros((pad,), F32)]).reshape(1, ROUTER_LANES)
        x1, h2r, mi, mf, cnt = _mix(
            attn_sinks[l], proj, x2, mod3, w_pool[l].astype(BF16),
            pool_scale[l].reshape(1, D_POOL), w_out[l].astype(BF16),
            g_norm2[l].reshape(1, d), w_r, b_r, bsz, seq)
        n_rows = 2 * t + N_EXPERTS * MOE_BLK
        counts = cnt[0, N_EXPERT_GROUPS:N_EXPERT_GROUPS + N_EXPERTS]
        pstarts, schedule = _block_schedule(counts, n_rows)
        xr, d1, d2 = _dispatch(pstarts, counts, mi[0], mi[1], mi[2], mi[3],
                               h2r.reshape(t, d // (2 * LANES), LANES), n_rows)
        yr = _experts(schedule, xr, w_gate[l], w_up[l], w_down[l])
        x2 = _combine(d1, d2, yr, x1, mf, mod3, g_final.reshape(1, d), seq)
    return x2.reshape(bsz, seq, d)
```

```python
import functools

import jax
import jax.numpy as jnp
from jax import lax
from jax.experimental import pallas as pl
from jax.experimental.pallas import tpu as pltpu

F32 = jnp.float32
BF16 = jnp.bfloat16
I32 = jnp.int32
U32 = jnp.uint32

D_POOL = 1024
POOL_WINDOWS = (2, 4, 8, 16)
POOL_GROUP = 256
MAX_POOL_WINDOW = 16
HEAD_DIM = 64
N_KV_HEADS = 2
GQA_GROUP = 8
WINDOW = 128
D_ATTN = 1024
D_KV = 128
N_EXPERT_GROUPS = 8
EXPERTS_PER_GROUP = 8
N_EXPERTS = 64
D_EXPERT = 512
N_MOD = 6
RMS_EPS = 1e-6
NEG_INF = -1e30

LANES = 128
SUBLANES = 8
HEADS_PER_VREG = LANES // HEAD_DIM
PAIRS_PER_KV = GQA_GROUP // HEADS_PER_VREG

ADA_TN = 1024
INPROJ_TM = 512
MIX_TQ = 512
MOE_BLK = 256
PAD_PIECES = tuple(MOE_BLK >> (k + 1) for k in range(MOE_BLK.bit_length() - 1))
DISPATCH_TC = 1024
COMBINE_TK = 256
DMA_UNROLL = 8
DMA_THREADS = 2
DISPATCH_UNROLL = 64
ROUTER_LANES = 128
META_ROWS = 8
VMEM_LIMIT = 56 * 1024 * 1024


def _silu(v):
    return v * jax.nn.sigmoid(v)


def _load_slabs(ref, rows):
    slabs = ref.shape[0] // rows
    words = [ref[pl.ds(s, rows, stride=slabs), :] for s in range(slabs)]
    return [pltpu.unpack_elementwise(w, index=half, packed_dtype=BF16, unpacked_dtype=F32)
            for half in range(2) for w in words]


def _store_slabs(ref, val):
    rows, d = val.shape
    slabs = ref.shape[0] // rows
    for s in range(slabs):
        lo = val[:, s * LANES:(s + 1) * LANES]
        hi = val[:, d // 2 + s * LANES:d // 2 + (s + 1) * LANES]
        ref[pl.ds(s, rows, stride=slabs), :] = pltpu.pack_elementwise([lo, hi], packed_dtype=BF16)


def _ada_kernel(c_ref, w_ref, b_ref, o_ref):
    ca = _silu(c_ref[...])
    o_ref[...] = jnp.dot(ca.astype(BF16), w_ref[...].astype(BF16),
                         preferred_element_type=F32) + b_ref[...]


def _ada(c, w, b):
    bsz, d = c.shape
    n = w.shape[1]
    return pl.pallas_call(
        _ada_kernel,
        out_shape=jax.ShapeDtypeStruct((bsz, n), F32),
        grid=(n // ADA_TN,),
        in_specs=[
            pl.BlockSpec((bsz, d), lambda i: (0, 0)),
            pl.BlockSpec((d, ADA_TN), lambda i: (0, i)),
            pl.BlockSpec((1, ADA_TN), lambda i: (0, i)),
        ],
        out_specs=pl.BlockSpec((bsz, ADA_TN), lambda i: (0, i)),
        compiler_params=pltpu.CompilerParams(
            dimension_semantics=("arbitrary",), vmem_limit_bytes=VMEM_LIMIT),
        name="ada",
    )(c, w, b.reshape(1, n))


def _norm_mod(x, g, shift, scale):
    ms = jnp.mean(x * x, axis=-1, keepdims=True)
    return (x * lax.rsqrt(ms + RMS_EPS)) * g * (1.0 + scale) + shift


def _inproj_kernel(x_ref, mod_ref, g_ref, w_ref, o_ref):
    h = _norm_mod(x_ref[...], g_ref[...], mod_ref[0, 0:1, :], mod_ref[0, 1:2, :])
    o_ref[...] = jnp.dot(h.astype(BF16), w_ref[...],
                         preferred_element_type=F32).astype(o_ref.dtype)


def _inproj(x2, mod3, g1, w_in, seq):
    t, d = x2.shape
    n = w_in.shape[1]
    per_b = seq // INPROJ_TM
    return pl.pallas_call(
        _inproj_kernel,
        out_shape=jax.ShapeDtypeStruct((t, n), BF16),
        grid=(t // INPROJ_TM,),
        in_specs=[
            pl.BlockSpec((INPROJ_TM, d), lambda i: (i, 0)),
            pl.BlockSpec((1, N_MOD, d), lambda i: (i // per_b, 0, 0)),
            pl.BlockSpec((1, d), lambda i: (0, 0)),
            pl.BlockSpec((d, n), lambda i: (0, 0)),
        ],
        out_specs=pl.BlockSpec((INPROJ_TM, n), lambda i: (i, 0)),
        compiler_params=pltpu.CompilerParams(
            dimension_semantics=("arbitrary",), vmem_limit_bytes=VMEM_LIMIT),
        name="inproj",
    )(x2, mod3, g1, w_in)


def _mix_kernel(sinks_ref, proj_ref, kvp_ref, up_ref, x_ref, mod_ref, wpool_ref,
                pscale_ref, wout_ref, g2_ref, wr_ref, br_ref,
                x1_ref, h2_ref, mi_ref, mf_ref, cnt_ref,
                ubuf, lvl_a, lvl_b, mixbuf, carry):
    b = pl.program_id(0)
    j = pl.program_id(1)
    tq = x_ref.shape[0]
    seq_start = j == 0

    @pl.when(jnp.logical_and(b == 0, j == 0))
    def _():
        carry[...] = jnp.zeros_like(carry)

    top = SUBLANES
    data = top + MAX_POOL_WINDOW
    ext = MAX_POOL_WINDOW + tq
    halo = up_ref[...].astype(F32)
    ubuf[0:top, :] = jnp.zeros((top, D_POOL), F32)
    ubuf[top:data, :] = jnp.where(seq_start, 0.0, halo)
    ubuf[data:data + tq, :] = proj_ref[:, 0:D_POOL].astype(F32)
    lvl_a[0:top, :] = jnp.zeros((top, POOL_GROUP), F32)
    lvl_b[0:top, :] = jnp.zeros((top, POOL_GROUP), F32)
    pos = j * tq + lax.broadcasted_iota(I32, (tq, 1), 0)
    for gi, w in enumerate(POOL_WINDOWS):
        c0 = gi * POOL_GROUP
        u = ubuf[data:data + tq, c0:c0 + POOL_GROUP]
        src, cols = ubuf, slice(c0, c0 + POOL_GROUP)
        span = 1
        while 2 * span < w:
            dst = lvl_b if src is lvl_a else lvl_a
            dst[top:top + ext, :] = src[top:top + ext, cols] + src[top - span:top - span + ext, cols]
            src, cols = dst, slice(0, POOL_GROUP)
            span *= 2
        acc = src[data:data + tq, cols] + src[data - span:data - span + tq, cols]
        cnt = jnp.minimum(pos + 1, w).astype(F32)
        delta = (acc / cnt - u).astype(BF16)
        yp = jnp.dot(delta, wpool_ref[gi], preferred_element_type=F32)
        yp = yp * pscale_ref[:, c0:c0 + POOL_GROUP]
        mixbuf[:, c0:c0 + POOL_GROUP] = yp.astype(BF16)

    lane = lax.broadcasted_iota(I32, (2 * WINDOW, LANES), 1)
    low = lane < HEAD_DIM
    qi = lax.broadcasted_iota(I32, (WINDOW, 2 * WINDOW), 0)
    kj = lax.broadcasted_iota(I32, (WINDOW, 2 * WINDOW), 1)
    dist = qi - kj + WINDOW
    band = jnp.logical_and(dist >= 0, dist < WINDOW)
    olane = lax.broadcasted_iota(I32, (WINDOW, LANES), 1)
    k_col = D_POOL + D_ATTN
    v_col = k_col + D_KV
    for blk in range(tq // WINDOW):
        r0 = blk * WINDOW
        if blk == 0:
            kcat = jnp.concatenate(
                [kvp_ref[:, 0:D_KV], proj_ref[0:WINDOW, k_col:k_col + D_KV]], axis=0)
            vcat = jnp.concatenate(
                [kvp_ref[:, D_KV:2 * D_KV], proj_ref[0:WINDOW, v_col:v_col + D_KV]], axis=0)
            mask = jnp.logical_and(band, kj >= jnp.where(seq_start, WINDOW, 0))
        else:
            kcat = proj_ref[r0 - WINDOW:r0 + WINDOW, k_col:k_col + D_KV]
            vcat = proj_ref[r0 - WINDOW:r0 + WINDOW, v_col:v_col + D_KV]
            mask = band
        kswap = pltpu.roll(kcat.astype(F32), HEAD_DIM, 1).astype(BF16)
        vswap = pltpu.roll(vcat.astype(F32), HEAD_DIM, 1).astype(BF16)
        zero = jnp.zeros_like(kcat)
        for g in range(N_KV_HEADS):
            ksrc_lo, ksrc_hi = (kcat, kswap) if g == 0 else (kswap, kcat)
            vsrc_lo, vsrc_hi = (vcat, vswap) if g == 0 else (vswap, vcat)
            kbd = jnp.concatenate([jnp.where(low, ksrc_lo, zero),
                                   jnp.where(low, zero, ksrc_hi)], axis=0)
            vbd = jnp.concatenate([jnp.where(low, vsrc_lo, zero),
                                   jnp.where(low, zero, vsrc_hi)], axis=0)
            q_col = D_POOL + g * GQA_GROUP * HEAD_DIM
            q = jnp.concatenate(
                [proj_ref[r0:r0 + WINDOW, q_col + p * LANES:q_col + (p + 1) * LANES]
                 for p in range(PAIRS_PER_KV)], axis=0)
            q = q * jnp.asarray(HEAD_DIM ** -0.5, BF16)
            s = lax.dot_general(q, kbd, (((1,), (1,)), ((), ())),
                                preferred_element_type=F32)
            probs = []
            rdens = []
            for p in range(PAIRS_PER_KV):
                row_p = []
                row_r = []
                for hh in range(HEADS_PER_VREG):
                    sink = sinks_ref[g * GQA_GROUP + p * HEADS_PER_VREG + hh]
                    sp = s[p * WINDOW:(p + 1) * WINDOW,
                           hh * 2 * WINDOW:(hh + 1) * 2 * WINDOW]
                    sp = jnp.where(mask, sp, NEG_INF)
                    m = jnp.maximum(jnp.max(sp, axis=-1, keepdims=True), sink)
                    e = jnp.exp(sp - m)
                    den = jnp.sum(e, axis=-1, keepdims=True) + jnp.exp(sink - m)
                    row_p.append(e.astype(BF16))
                    row_r.append(1.0 / den)
                probs.append(jnp.concatenate(row_p, axis=1))
                rdens.append(row_r)
            pmat = jnp.concatenate(probs, axis=0)
            o = jnp.dot(pmat, vbd, preferred_element_type=F32)
            for p in range(PAIRS_PER_KV):
                op = o[p * WINDOW:(p + 1) * WINDOW, :]
                norm = jnp.where(olane < HEAD_DIM, rdens[p][0], rdens[p][1])
                c0 = D_POOL + g * GQA_GROUP * HEAD_DIM + p * LANES
                mixbuf[r0:r0 + WINDOW, c0:c0 + LANES] = (op * norm).astype(BF16)

    y = jnp.dot(mixbuf[...], wout_ref[...], preferred_element_type=F32)
    x1 = x_ref[...] + mod_ref[0, 2:3, :] * y
    x1_ref[...] = x1

    h2 = _norm_mod(x1, g2_ref[...], mod_ref[0, 3:4, :], mod_ref[0, 4:5, :])
    _store_slabs(h2_ref, h2)

    logits = jnp.dot(h2.astype(BF16), wr_ref[...], preferred_element_type=F32) + br_ref[...]
    ln = lax.broadcasted_iota(I32, (tq, ROUTER_LANES), 1)
    lnf = ln.astype(F32)
    ninf = -jnp.inf
    is_g = ln < N_EXPERT_GROUPS
    gl = jnp.where(is_g, logits, ninf)
    gmax = jnp.max(gl, axis=-1, keepdims=True)
    g_sel = jnp.min(jnp.where(gl == gmax, lnf, float(ROUTER_LANES)), axis=-1, keepdims=True)
    p_grp = 1.0 / jnp.sum(jnp.where(is_g, jnp.exp(logits - gmax), 0.0), axis=-1, keepdims=True)
    lane_grp = ((ln - N_EXPERT_GROUPS) >> 3).astype(F32)
    in_sel = jnp.logical_and(
        jnp.logical_and(ln >= N_EXPERT_GROUPS, ln < N_EXPERT_GROUPS + N_EXPERTS),
        lane_grp == g_sel)
    el = jnp.where(in_sel, logits, ninf)
    l1 = jnp.max(el, axis=-1, keepdims=True)
    i1 = jnp.min(jnp.where(el == l1, lnf, float(ROUTER_LANES)), axis=-1, keepdims=True)
    el2 = jnp.where(lnf == i1, ninf, el)
    l2 = jnp.max(el2, axis=-1, keepdims=True)
    i2 = jnp.min(jnp.where(el2 == l2, lnf, float(ROUTER_LANES)), axis=-1, keepdims=True)
    tt = jnp.exp(l2 - l1)
    w1 = 1.0 / (1.0 + tt)
    comb1 = p_grp * w1
    comb2 = p_grp * (tt * w1)
    hit1 = lnf == i1
    hit2 = lnf == i2
    onehot = jnp.where(jnp.logical_or(hit1, hit2), 1.0, 0.0)
    rr = lax.broadcasted_iota(I32, (tq, tq), 0)
    cc = lax.broadcasted_iota(I32, (tq, tq), 1)
    lower = jnp.where(rr > cc, 1.0, 0.0).astype(BF16)
    prior = jnp.dot(lower, onehot.astype(BF16), preferred_element_type=F32) + carry[...]
    rank1 = jnp.sum(jnp.where(hit1, prior, 0.0), axis=-1, keepdims=True)
    rank2 = jnp.sum(jnp.where(hit2, prior, 0.0), axis=-1, keepdims=True)
    carry[...] = carry[...] + jnp.sum(onehot, axis=0, keepdims=True)
    cnt_ref[...] = carry[...].astype(I32)
    eid1 = i1 - float(N_EXPERT_GROUPS)
    eid2 = i2 - float(N_EXPERT_GROUPS)
    meta = jnp.where(ln == 0, eid1, jnp.where(ln == 1, eid2,
                     jnp.where(ln == 2, rank1, jnp.where(ln == 3, rank2, 0.0))))
    mi_ref[...] = meta.T[0:META_ROWS, :].astype(I32)
    mf_ref[...] = jnp.where(ln == 0, comb1, jnp.where(ln == 1, comb2, 0.0))


def _mix(sinks, proj, x2, mod3, w_pool, pool_scale, w_out, g2, w_r, b_r, bsz, seq):
    t, d = x2.shape
    n_in = proj.shape[1]
    per_b = seq // MIX_TQ
    q_per_win = MIX_TQ // WINDOW
    q_per_halo = MIX_TQ // MAX_POOL_WINDOW
    slabs = d // (2 * LANES)

    def row(b, j):
        return b * per_b + j

    out_shapes = (
        jax.ShapeDtypeStruct((t, d), F32),
        jax.ShapeDtypeStruct((t * slabs, LANES), U32),
        jax.ShapeDtypeStruct((META_ROWS, t), I32),
        jax.ShapeDtypeStruct((t, ROUTER_LANES), F32),
        jax.ShapeDtypeStruct((1, ROUTER_LANES), I32),
    )
    return pl.pallas_call(
        _mix_kernel,
        out_shape=out_shapes,
        grid=(bsz, per_b),
        in_specs=[
            pl.BlockSpec(memory_space=pltpu.SMEM),
            pl.BlockSpec((MIX_TQ, n_in), lambda b, j: (row(b, j), 0)),
            pl.BlockSpec((WINDOW, 2 * D_KV),
                         lambda b, j: (jnp.maximum(row(b, j) * q_per_win - 1, 0),
                                       (D_POOL + D_ATTN) // (2 * D_KV))),
            pl.BlockSpec((MAX_POOL_WINDOW, D_POOL),
                         lambda b, j: (jnp.maximum(row(b, j) * q_per_halo - 1, 0), 0)),
            pl.BlockSpec((MIX_TQ, d), lambda b, j: (row(b, j), 0)),
            pl.BlockSpec((1, N_MOD, d), lambda b, j: (b, 0, 0)),
            pl.BlockSpec(w_pool.shape, lambda b, j: (0, 0, 0)),
            pl.BlockSpec((1, D_POOL), lambda b, j: (0, 0)),
            pl.BlockSpec((d, d), lambda b, j: (0, 0)),
            pl.BlockSpec((1, d), lambda b, j: (0, 0)),
            pl.BlockSpec((d, ROUTER_LANES), lambda b, j: (0, 0)),
            pl.BlockSpec((1, ROUTER_LANES), lambda b, j: (0, 0)),
        ],
        out_specs=(
            pl.BlockSpec((MIX_TQ, d), lambda b, j: (row(b, j), 0)),
            pl.BlockSpec((MIX_TQ * slabs, LANES), lambda b, j: (row(b, j), 0)),
            pl.BlockSpec((META_ROWS, MIX_TQ), lambda b, j: (0, row(b, j))),
            pl.BlockSpec((MIX_TQ, ROUTER_LANES), lambda b, j: (row(b, j), 0)),
            pl.BlockSpec((1, ROUTER_LANES), lambda b, j: (0, 0)),
        ),
        scratch_shapes=[
            pltpu.VMEM((SUBLANES + MAX_POOL_WINDOW + MIX_TQ, D_POOL), F32),
            pltpu.VMEM((SUBLANES + MAX_POOL_WINDOW + MIX_TQ, POOL_GROUP), F32),
            pltpu.VMEM((SUBLANES + MAX_POOL_WINDOW + MIX_TQ, POOL_GROUP), F32),
            pltpu.VMEM((MIX_TQ, d), BF16),
            pltpu.VMEM((1, ROUTER_LANES), F32),
        ],
        compiler_params=pltpu.CompilerParams(
            dimension_semantics=("arbitrary", "arbitrary"), vmem_limit_bytes=VMEM_LIMIT),
        name="mix",
    )(sinks, proj, proj, proj, x2, mod3, w_pool, pool_scale, w_out, g2, w_r, b_r)


def _dest_kernel(pstart_ref, mi_ref, o_ref):
    e1 = mi_ref[0:1, :]
    e2 = mi_ref[1:2, :]
    base1 = jnp.zeros_like(e1)
    base2 = jnp.zeros_like(e2)
    for e in range(N_EXPERTS):
        start = pstart_ref[e]
        base1 = jnp.where(e1 == e, start, base1)
        base2 = jnp.where(e2 == e, start, base2)
    fields = lax.broadcasted_iota(I32, o_ref.shape, 0)
    o_ref[...] = jnp.where(fields == 0, base1 + mi_ref[2:3, :],
                           jnp.where(fields == 1, base2 + mi_ref[3:4, :], 0))


def _dest(pstarts, mi):
    return pl.pallas_call(
        _dest_kernel,
        out_shape=jax.ShapeDtypeStruct(mi.shape, I32),
        in_specs=[pl.BlockSpec(memory_space=pltpu.SMEM),
                  pl.BlockSpec(mi.shape, lambda: (0, 0))],
        out_specs=pl.BlockSpec(mi.shape, lambda: (0, 0)),
        name="dest",
    )(pstarts, mi)


def _dispatch_kernel(pstart_ref, count_ref, d1_ref, d2_ref, h2_ref, xr_hbm, zbuf, sem, zsem):
    tc = d1_ref.shape[0]
    step = pl.program_id(0)
    nblk = xr_hbm.shape[0] // MOE_BLK
    last = N_EXPERTS - 1
    n_used = (pstart_ref[last] + count_ref[last] + MOE_BLK - 1) // MOE_BLK

    def pad_copies(e):
        cnt = count_ref[e]
        npad = (-cnt) & (MOE_BLK - 1)
        off = pstart_ref[e] + cnt
        out = []
        for piece in PAD_PIECES:
            out.append((npad & piece, pltpu.make_async_copy(
                zbuf.at[pl.ds(0, piece)], xr_hbm.at[pl.ds(off, piece)], zsem)))
            off = off + (npad & piece)
        return out

    def tail_copy(blk):
        return pltpu.make_async_copy(zbuf, xr_hbm.at[pl.ds(blk * MOE_BLK, MOE_BLK)], zsem)

    def zero_fill(start):
        def per_expert(e, carry):
            for flag, cp in pad_copies(e):
                pl.when(flag != 0)(cp.start if start else cp.wait)
            return carry

        def per_tail(blk, carry):
            cp = tail_copy(blk)
            cp.start() if start else cp.wait()
            return carry

        lax.fori_loop(0, N_EXPERTS, per_expert, 0)
        lax.fori_loop(n_used, nblk, per_tail, 0)

    @pl.when(step == 0)
    def _():
        zbuf[...] = jnp.zeros_like(zbuf)
        zero_fill(True)

    def copies(tl):
        src = h2_ref.at[tl]
        return (pltpu.make_async_copy(src, xr_hbm.at[d1_ref[tl]], sem),
                pltpu.make_async_copy(src, xr_hbm.at[d2_ref[tl]], sem))

    def issue(i, carry):
        for u in range(DISPATCH_UNROLL):
            for k, cp in enumerate(copies(i * DISPATCH_UNROLL + u)):
                cp.start(priority=(2 * u + k) % DMA_THREADS)
        return carry

    def drain(i, carry):
        for u in range(DMA_UNROLL):
            for cp in copies(i * DMA_UNROLL + u):
                cp.wait()
        return carry

    lax.fori_loop(0, tc // DISPATCH_UNROLL, issue, 0)
    lax.fori_loop(0, tc // DMA_UNROLL, drain, 0)

    @pl.when(step == 0)
    def _():
        zero_fill(False)


def _dispatch(pstarts, counts, d1, d2, h2r, n_rows):
    t = d1.shape[0]
    tc = min(DISPATCH_TC, t)
    smem_blk = pl.BlockSpec((tc,), lambda i: (i,), memory_space=pltpu.SMEM)
    return pl.pallas_call(
        _dispatch_kernel,
        out_shape=jax.ShapeDtypeStruct((n_rows,) + h2r.shape[1:], h2r.dtype),
        grid=(t // tc,),
        in_specs=[
            pl.BlockSpec(memory_space=pltpu.SMEM),
            pl.BlockSpec(memory_space=pltpu.SMEM),
            smem_blk, smem_blk,
            pl.BlockSpec((tc,) + h2r.shape[1:], lambda i: (i, 0, 0)),
        ],
        out_specs=pl.BlockSpec(memory_space=pl.ANY),
        scratch_shapes=[pltpu.VMEM((MOE_BLK,) + h2r.shape[1:], h2r.dtype),
                        pltpu.SemaphoreType.DMA(()), pltpu.SemaphoreType.DMA(())],
        compiler_params=pltpu.CompilerParams(
            dimension_semantics=("arbitrary",), has_side_effects=True,
            vmem_limit_bytes=VMEM_LIMIT),
        name="dispatch",
    )(pstarts, counts, d1, d2, h2r)


def _experts_kernel(blk_e_ref, blk_row_ref, blk_valid_ref, blk_first_ref, blk_slot_ref,
                    blk_next_ref, xr_ref, wg_hbm, wu_hbm, wd_hbm, yr_ref,
                    wg_f32, wu_f32, wd_f32, wg_bf, wu_bf, wd_bf, xb, wsems):
    i = pl.program_id(0)
    nvalid = blk_valid_ref[i]
    slot = blk_slot_ref[i]

    def weight_copies(expert, s):
        return [pltpu.make_async_copy(src.at[expert], dst.at[s], wsems.at[s])
                for src, dst in ((wg_hbm, wg_f32), (wu_hbm, wu_f32), (wd_hbm, wd_f32))]

    @pl.when(i == 0)
    def _():
        for cp in weight_copies(blk_e_ref[0], slot):
            cp.start()

    @pl.when(blk_first_ref[i] == 1)
    def _():
        nxt = blk_next_ref[i]

        @pl.when(nxt >= 0)
        def _():
            for cp in weight_copies(nxt, 1 - slot):
                cp.start()

        for cp in weight_copies(blk_e_ref[i], slot):
            cp.wait()
        wg_bf[...] = wg_f32[slot].astype(BF16)
        wu_bf[...] = wu_f32[slot].astype(BF16)
        wd_bf[...] = wd_f32[slot].astype(BF16)

    @pl.when(nvalid > 0)
    def _():
        rows = xb.shape[0]
        for sidx, slab in enumerate(_load_slabs(xr_ref, rows)):
            xb[:, sidx * LANES:(sidx + 1) * LANES] = slab.astype(BF16)
        xv = xb[...]
        gate = jnp.dot(xv, wg_bf[...], preferred_element_type=F32)
        up = jnp.dot(xv, wu_bf[...], preferred_element_type=F32)
        act = (_silu(gate) * up).astype(BF16)
        _store_slabs(yr_ref, jnp.dot(act, wd_bf[...], preferred_element_type=F32))

    @pl.when(nvalid == 0)
    def _():
        _store_slabs(yr_ref, jnp.zeros((xb.shape[0], xb.shape[1]), F32))


def _experts(schedule, xr, w_gate, w_up, w_down):
    n_rows, slabs, _ = xr.shape
    d = 2 * slabs * LANES
    nblk = n_rows // MOE_BLK
    grid_spec = pltpu.PrefetchScalarGridSpec(
        num_scalar_prefetch=len(schedule),
        grid=(nblk,),
        in_specs=[
            pl.BlockSpec((MOE_BLK * slabs, LANES), lambda i, *sched: (sched[1][i], 0)),
            pl.BlockSpec(memory_space=pl.ANY),
            pl.BlockSpec(memory_space=pl.ANY),
            pl.BlockSpec(memory_space=pl.ANY),
        ],
        out_specs=pl.BlockSpec((MOE_BLK * slabs, LANES), lambda i, *sched: (i, 0)),
        scratch_shapes=[
            pltpu.VMEM((2, d, D_EXPERT), F32),
            pltpu.VMEM((2, d, D_EXPERT), F32),
            pltpu.VMEM((2, D_EXPERT, d), F32),
            pltpu.VMEM((d, D_EXPERT), BF16),
            pltpu.VMEM((d, D_EXPERT), BF16),
            pltpu.VMEM((D_EXPERT, d), BF16),
            pltpu.VMEM((MOE_BLK, d), BF16),
            pltpu.SemaphoreType.DMA((2,)),
        ],
    )
    yr = pl.pallas_call(
        _experts_kernel,
        out_shape=jax.ShapeDtypeStruct((n_rows * slabs, LANES), U32),
        grid_spec=grid_spec,
        compiler_params=pltpu.CompilerParams(
            dimension_semantics=("arbitrary",), vmem_limit_bytes=VMEM_LIMIT),
        name="experts",
    )(*schedule, xr.reshape(n_rows * slabs, LANES), w_gate, w_up, w_down)
    return yr.reshape(n_rows, slabs, LANES)


def _combine_kernel(d1_ref, d2_ref, yr_hbm, x1_ref, mf_ref, mod_ref, gf_ref, o_ref,
                    ya0, yb0, ya1, yb1, sems):
    i = pl.program_id(0)
    n = pl.num_programs(0)
    tk = x1_ref.shape[0]
    bufs = ((ya0, yb0), (ya1, yb1))

    slabs = yr_hbm.shape[1]

    def copies(tile, slot, tl):
        tok = tile * tk + tl
        ya, yb = bufs[slot]
        dst = pl.ds(pl.multiple_of(tl * slabs, slabs), slabs)
        return (pltpu.make_async_copy(yr_hbm.at[d1_ref[tok]], ya.at[dst], sems.at[slot]),
                pltpu.make_async_copy(yr_hbm.at[d2_ref[tok]], yb.at[dst], sems.at[slot]))

    def issue(tile, slot):
        def body(k, carry):
            for u in range(DMA_UNROLL):
                for c, cp in enumerate(copies(tile, slot, k * DMA_UNROLL + u)):
                    cp.start(priority=(2 * u + c) % DMA_THREADS)
            return carry
        lax.fori_loop(0, tk // DMA_UNROLL, body, 0)

    def drain(tile, slot):
        def body(k, carry):
            for u in range(DMA_UNROLL):
                for cp in copies(tile, slot, k * DMA_UNROLL + u):
                    cp.wait()
            return carry
        lax.fori_loop(0, tk // DMA_UNROLL, body, 0)

    def step(slot):
        @pl.when(i == 0)
        def _():
            issue(i, slot)

        drain(i, slot)
        nxt = jnp.minimum(i + 1, n - 1)
        for tl in range(tk):
            for c, cp in enumerate(copies(nxt, 1 - slot, tl)):
                cp.start(priority=(2 * tl + c) % DMA_THREADS)
        ya, yb = bufs[slot]
        c1 = mf_ref[:, 0:1]
        c2 = mf_ref[:, 1:2]
        moe = jnp.concatenate(
            [c1 * sa + c2 * sb for sa, sb in zip(_load_slabs(ya, tk), _load_slabs(yb, tk))], axis=1)
        xo = x1_ref[...] + mod_ref[0, 5:6, :] * moe
        ms = jnp.mean(xo * xo, axis=-1, keepdims=True)
        o_ref[...] = (xo * lax.rsqrt(ms + RMS_EPS)) * gf_ref[...]

        @pl.when(i == n - 1)
        def _():
            drain(nxt, 1 - slot)

    @pl.when(i % 2 == 0)
    def _():
        step(0)

    @pl.when(i % 2 == 1)
    def _():
        step(1)


def _combine(d1, d2, yr, x1, mf, mod3, g_final, seq):
    t, d = x1.shape
    slabs = yr.shape[1]
    per_b = seq // COMBINE_TK
    grid_spec = pltpu.PrefetchScalarGridSpec(
        num_scalar_prefetch=2,
        grid=(t // COMBINE_TK,),
        in_specs=[
            pl.BlockSpec(memory_space=pl.ANY),
            pl.BlockSpec((COMBINE_TK, d), lambda i, a, b: (i, 0)),
            pl.BlockSpec((COMBINE_TK, ROUTER_LANES), lambda i, a, b: (i, 0)),
            pl.BlockSpec((1, N_MOD, d), lambda i, a, b: (i // per_b, 0, 0)),
            pl.BlockSpec((1, d), lambda i, a, b: (0, 0)),
        ],
        out_specs=pl.BlockSpec((COMBINE_TK, d), lambda i, a, b: (i, 0)),
        scratch_shapes=[pltpu.VMEM((COMBINE_TK * slabs, LANES), U32) for _ in range(4)]
        + [pltpu.SemaphoreType.DMA((2,))],
    )
    return pl.pallas_call(
        _combine_kernel,
        out_shape=jax.ShapeDtypeStruct((t, d), F32),
        grid_spec=grid_spec,
        compiler_params=pltpu.CompilerParams(
            dimension_semantics=("arbitrary",), vmem_limit_bytes=VMEM_LIMIT),
        name="combine",
    )(d1, d2, yr, x1, mf, mod3, g_final)


def _block_schedule(counts, n_rows):
    nblk = n_rows // MOE_BLK
    eidx = jnp.arange(N_EXPERTS, dtype=I32)
    bidx = jnp.arange(nblk, dtype=I32)
    nblocks = (counts + MOE_BLK - 1) // MOE_BLK
    bends = jnp.cumsum(nblocks)
    bstarts = bends - nblocks
    n_used = bends[-1]
    used = bidx < n_used
    blk_row = jnp.minimum(bidx, jnp.maximum(n_used - 1, 0))
    owner = jnp.logical_and(blk_row[:, None] >= bstarts[None, :], blk_row[:, None] < bends[None, :])

    def pick(per_expert):
        return jnp.sum(jnp.where(owner, per_expert[None, :], 0), axis=1).astype(I32)

    blk_e = pick(eidx)
    blk_valid = jnp.where(
        used, jnp.clip(pick(counts) - (bidx - pick(bstarts)) * MOE_BLK, 0, MOE_BLK), 0).astype(I32)
    blk_first = jnp.logical_and(used, bidx == pick(bstarts)).astype(I32)
    nonempty = counts > 0
    blk_slot = pick(jnp.cumsum(nonempty.astype(I32)) - 1) & 1
    later = jnp.logical_and(nonempty[None, :], eidx[None, :] > eidx[:, None])
    nxt = jnp.min(jnp.where(later, eidx[None, :], N_EXPERTS), axis=1)
    blk_next = pick(jnp.where(nxt < N_EXPERTS, nxt, -1))
    pstarts = (bstarts * MOE_BLK).astype(I32)
    return pstarts, (blk_e, blk_row, blk_valid, blk_first, blk_slot, blk_next)


def kernel(x, c, w_ada, b_ada, g_norm1, w_in, w_pool, pool_scale, attn_sinks, w_out,
           g_norm2, w_router_group, b_router_group, w_router_expert, b_router_expert,
           w_gate, w_up, w_down, g_final):
    bsz, seq, d = x.shape
    t = bsz * seq
    assert w_ada.shape[0] == 1, "single-layer model: the combine step applies the final norm"
    x2 = x.reshape(t, d)
    for l in range(1):
        mod3 = _ada(c, w_ada[l], b_ada[l]).reshape(bsz, N_MOD, d)
        proj = _inproj(x2, mod3, g_norm1[l].reshape(1, d), w_in[l].astype(BF16), seq)
        pad = ROUTER_LANES - N_EXPERT_GROUPS - N_EXPERTS
        w_r = jnp.concatenate(
            [w_router_group[l], w_router_expert[l], jnp.zeros((d, pad), F32)], axis=1).astype(BF16)
        b_r = jnp.concatenate(
            [b_router_group[l], b_router_expert[l], jnp.zeros((pad,), F32)]).reshape(1, ROUTER_LANES)
        x1, h2r, mi, mf, cnt = _mix(
            attn_sinks[l], proj, x2, mod3, w_pool[l].astype(BF16),
            pool_scale[l].reshape(1, D_POOL), w_out[l].astype(BF16),
            g_norm2[l].reshape(1, d), w_r, b_r, bsz, seq)
        n_rows = 2 * t + N_EXPERTS * MOE_BLK
        counts = cnt[0, N_EXPERT_GROUPS:N_EXPERT_GROUPS + N_EXPERTS]
        pstarts, schedule = _block_schedule(counts, n_rows)
        dest = _dest(pstarts, mi)
        d1, d2 = dest[0], dest[1]
        xr = _dispatch(pstarts, counts, d1, d2, h2r.reshape(t, d // (2 * LANES), LANES), n_rows)
        yr = _experts(schedule, xr, w_gate[l], w_up[l], w_down[l])
        x2 = _combine(d1, d2, yr, x1, mf, mod3, g_final.reshape(1, d), seq)
    return x2.reshape(bsz, seq, d)
```

```python
import functools

import jax
import jax.numpy as jnp
from jax import lax
from jax.experimental import pallas as pl
from jax.experimental.pallas import tpu as pltpu

F32 = jnp.float32
BF16 = jnp.bfloat16
I32 = jnp.int32
U32 = jnp.uint32

D_POOL = 1024
POOL_WINDOWS = (2, 4, 8, 16)
POOL_GROUP = 256
MAX_POOL_WINDOW = 16
HEAD_DIM = 64
N_KV_HEADS = 2
GQA_GROUP = 8
WINDOW = 128
D_ATTN = 1024
D_KV = 128
N_EXPERT_GROUPS = 8
EXPERTS_PER_GROUP = 8
N_EXPERTS = 64
D_EXPERT = 512
N_MOD = 6
RMS_EPS = 1e-6
NEG_INF = -1e30

LANES = 128
SUBLANES = 8
HEADS_PER_VREG = LANES // HEAD_DIM
PAIRS_PER_KV = GQA_GROUP // HEADS_PER_VREG

ADA_TN = 1024
INPROJ_TM = 1024
MIX_TQ = 512
MOE_BLK = 256
WEIGHT_SLOTS = 3
PAD_PIECES = tuple(MOE_BLK >> (k + 1) for k in range(MOE_BLK.bit_length() - 1))
DISPATCH_TC = 1024
COMBINE_TK = 256
DMA_UNROLL = 8
DMA_THREADS = 2
DISPATCH_UNROLL = 64
ROUTER_LANES = 128
META_ROWS = 8
VMEM_LIMIT = 56 * 1024 * 1024


def _silu(v):
    return v * jax.nn.sigmoid(v)


def _load_slabs(ref, rows):
    slabs = ref.shape[0] // rows
    words = [ref[pl.ds(s, rows, stride=slabs), :] for s in range(slabs)]
    return [pltpu.unpack_elementwise(w, index=half, packed_dtype=BF16, unpacked_dtype=F32)
            for half in range(2) for w in words]


def _store_slabs(ref, val):
    rows, d = val.shape
    slabs = ref.shape[0] // rows
    for s in range(slabs):
        lo = val[:, s * LANES:(s + 1) * LANES]
        hi = val[:, d // 2 + s * LANES:d // 2 + (s + 1) * LANES]
        ref[pl.ds(s, rows, stride=slabs), :] = pltpu.pack_elementwise([lo, hi], packed_dtype=BF16)


def _ada_kernel(c_ref, w_ref, b_ref, o_ref):
    ca = _silu(c_ref[...])
    o_ref[...] = jnp.dot(ca.astype(BF16), w_ref[...].astype(BF16),
                         preferred_element_type=F32) + b_ref[...]


def _ada(c, w, b):
    bsz, d = c.shape
    n = w.shape[1]
    return pl.pallas_call(
        _ada_kernel,
        out_shape=jax.ShapeDtypeStruct((bsz, n), F32),
        grid=(n // ADA_TN,),
        in_specs=[
            pl.BlockSpec((bsz, d), lambda i: (0, 0)),
            pl.BlockSpec((d, ADA_TN), lambda i: (0, i)),
            pl.BlockSpec((1, ADA_TN), lambda i: (0, i)),
        ],
        out_specs=pl.BlockSpec((bsz, ADA_TN), lambda i: (0, i)),
        compiler_params=pltpu.CompilerParams(
            dimension_semantics=("arbitrary",), vmem_limit_bytes=VMEM_LIMIT),
        name="ada",
    )(c, w, b.reshape(1, n))


def _norm_mod(x, g, shift, scale):
    ms = jnp.mean(x * x, axis=-1, keepdims=True)
    return (x * lax.rsqrt(ms + RMS_EPS)) * g * (1.0 + scale) + shift


def _inproj_kernel(x_ref, mod_ref, g_ref, w_ref, o_ref):
    h = _norm_mod(x_ref[...], g_ref[...], mod_ref[0, 0:1, :], mod_ref[0, 1:2, :])
    o_ref[...] = jnp.dot(h.astype(BF16), w_ref[...],
                         preferred_element_type=F32).astype(o_ref.dtype)


def _inproj(x2, mod3, g1, w_in, seq):
    t, d = x2.shape
    n = w_in.shape[1]
    per_b = seq // INPROJ_TM
    return pl.pallas_call(
        _inproj_kernel,
        out_shape=jax.ShapeDtypeStruct((t, n), BF16),
        grid=(t // INPROJ_TM,),
        in_specs=[
            pl.BlockSpec((INPROJ_TM, d), lambda i: (i, 0)),
            pl.BlockSpec((1, N_MOD, d), lambda i: (i // per_b, 0, 0)),
            pl.BlockSpec((1, d), lambda i: (0, 0)),
            pl.BlockSpec((d, n), lambda i: (0, 0)),
        ],
        out_specs=pl.BlockSpec((INPROJ_TM, n), lambda i: (i, 0)),
        compiler_params=pltpu.CompilerParams(
            dimension_semantics=("arbitrary",), vmem_limit_bytes=VMEM_LIMIT),
        name="inproj",
    )(x2, mod3, g1, w_in)


def _mix_kernel(sinks_ref, proj_ref, kvp_ref, up_ref, x_ref, mod_ref, wpool_ref,
                pscale_ref, wout_ref, g2_ref, wr_ref, br_ref,
                x1_ref, h2_ref, mi_ref, mf_ref, cnt_ref,
                ubuf, lvl_a, lvl_b, mixbuf, carry):
    b = pl.program_id(0)
    j = pl.program_id(1)
    tq = x_ref.shape[0]
    seq_start = j == 0

    @pl.when(jnp.logical_and(b == 0, j == 0))
    def _():
        carry[...] = jnp.zeros_like(carry)

    top = SUBLANES
    data = top + MAX_POOL_WINDOW
    ext = MAX_POOL_WINDOW + tq
    halo = up_ref[...].astype(F32)
    ubuf[0:top, :] = jnp.zeros((top, D_POOL), F32)
    ubuf[top:data, :] = jnp.where(seq_start, 0.0, halo)
    ubuf[data:data + tq, :] = proj_ref[:, 0:D_POOL].astype(F32)
    lvl_a[0:top, :] = jnp.zeros((top, POOL_GROUP), F32)
    lvl_b[0:top, :] = jnp.zeros((top, POOL_GROUP), F32)
    pos = j * tq + lax.broadcasted_iota(I32, (tq, 1), 0)
    for gi, w in enumerate(POOL_WINDOWS):
        c0 = gi * POOL_GROUP
        u = ubuf[data:data + tq, c0:c0 + POOL_GROUP]
        src, cols = ubuf, slice(c0, c0 + POOL_GROUP)
        span = 1
        while 2 * span < w:
            dst = lvl_b if src is lvl_a else lvl_a
            dst[top:top + ext, :] = src[top:top + ext, cols] + src[top - span:top - span + ext, cols]
            src, cols = dst, slice(0, POOL_GROUP)
            span *= 2
        acc = src[data:data + tq, cols] + src[data - span:data - span + tq, cols]
        cnt = jnp.minimum(pos + 1, w).astype(F32)
        delta = (acc / cnt - u).astype(BF16)
        yp = jnp.dot(delta, wpool_ref[gi], preferred_element_type=F32)
        yp = yp * pscale_ref[:, c0:c0 + POOL_GROUP]
        mixbuf[:, c0:c0 + POOL_GROUP] = yp.astype(BF16)

    lane = lax.broadcasted_iota(I32, (2 * WINDOW, LANES), 1)
    low = lane < HEAD_DIM
    qi = lax.broadcasted_iota(I32, (WINDOW, 2 * WINDOW), 0)
    kj = lax.broadcasted_iota(I32, (WINDOW, 2 * WINDOW), 1)
    dist = qi - kj + WINDOW
    band = jnp.logical_and(dist >= 0, dist < WINDOW)
    olane = lax.broadcasted_iota(I32, (WINDOW, LANES), 1)
    k_col = D_POOL + D_ATTN
    v_col = k_col + D_KV
    for blk in range(tq // WINDOW):
        r0 = blk * WINDOW
        if blk == 0:
            kcat = jnp.concatenate(
                [kvp_ref[:, 0:D_KV], proj_ref[0:WINDOW, k_col:k_col + D_KV]], axis=0)
            vcat = jnp.concatenate(
                [kvp_ref[:, D_KV:2 * D_KV], proj_ref[0:WINDOW, v_col:v_col + D_KV]], axis=0)
            mask = jnp.logical_and(band, kj >= jnp.where(seq_start, WINDOW, 0))
        else:
            kcat = proj_ref[r0 - WINDOW:r0 + WINDOW, k_col:k_col + D_KV]
            vcat = proj_ref[r0 - WINDOW:r0 + WINDOW, v_col:v_col + D_KV]
            mask = band
        kswap = pltpu.roll(kcat.astype(F32), HEAD_DIM, 1).astype(BF16)
        vswap = pltpu.roll(vcat.astype(F32), HEAD_DIM, 1).astype(BF16)
        zero = jnp.zeros_like(kcat)
        for g in range(N_KV_HEADS):
            ksrc_lo, ksrc_hi = (kcat, kswap) if g == 0 else (kswap, kcat)
            vsrc_lo, vsrc_hi = (vcat, vswap) if g == 0 else (vswap, vcat)
            kbd = jnp.concatenate([jnp.where(low, ksrc_lo, zero),
                                   jnp.where(low, zero, ksrc_hi)], axis=0)
            vbd = jnp.concatenate([jnp.where(low, vsrc_lo, zero),
                                   jnp.where(low, zero, vsrc_hi)], axis=0)
            q_col = D_POOL + g * GQA_GROUP * HEAD_DIM
            q = jnp.concatenate(
                [proj_ref[r0:r0 + WINDOW, q_col + p * LANES:q_col + (p + 1) * LANES]
                 for p in range(PAIRS_PER_KV)], axis=0)
            q = q * jnp.asarray(HEAD_DIM ** -0.5, BF16)
            s = lax.dot_general(q, kbd, (((1,), (1,)), ((), ())),
                                preferred_element_type=F32)
            probs = []
            rdens = []
            for p in range(PAIRS_PER_KV):
                row_p = []
                row_r = []
                for hh in range(HEADS_PER_VREG):
                    sink = sinks_ref[g * GQA_GROUP + p * HEADS_PER_VREG + hh]
                    sp = s[p * WINDOW:(p + 1) * WINDOW,
                           hh * 2 * WINDOW:(hh + 1) * 2 * WINDOW]
                    sp = jnp.where(mask, sp, NEG_INF)
                    m = jnp.maximum(jnp.max(sp, axis=-1, keepdims=True), sink)
                    e = jnp.exp(sp - m)
                    den = jnp.sum(e, axis=-1, keepdims=True) + jnp.exp(sink - m)
                    row_p.append(e.astype(BF16))
                    row_r.append(1.0 / den)
                probs.append(jnp.concatenate(row_p, axis=1))
                rdens.append(row_r)
            pmat = jnp.concatenate(probs, axis=0)
            o = jnp.dot(pmat, vbd, preferred_element_type=F32)
            for p in range(PAIRS_PER_KV):
                op = o[p * WINDOW:(p + 1) * WINDOW, :]
                norm = jnp.where(olane < HEAD_DIM, rdens[p][0], rdens[p][1])
                c0 = D_POOL + g * GQA_GROUP * HEAD_DIM + p * LANES
                mixbuf[r0:r0 + WINDOW, c0:c0 + LANES] = (op * norm).astype(BF16)

    y = jnp.dot(mixbuf[...], wout_ref[...], preferred_element_type=F32)
    x1 = x_ref[...] + mod_ref[0, 2:3, :] * y
    x1_ref[...] = x1

    h2 = _norm_mod(x1, g2_ref[...], mod_ref[0, 3:4, :], mod_ref[0, 4:5, :])
    _store_slabs(h2_ref, h2)

    logits = jnp.dot(h2.astype(BF16), wr_ref[...], preferred_element_type=F32) + br_ref[...]
    ln = lax.broadcasted_iota(I32, (tq, ROUTER_LANES), 1)
    lnf = ln.astype(F32)
    ninf = -jnp.inf
    is_g = ln < N_EXPERT_GROUPS
    gl = jnp.where(is_g, logits, ninf)
    gmax = jnp.max(gl, axis=-1, keepdims=True)
    g_sel = jnp.min(jnp.where(gl == gmax, lnf, float(ROUTER_LANES)), axis=-1, keepdims=True)
    p_grp = 1.0 / jnp.sum(jnp.where(is_g, jnp.exp(logits - gmax), 0.0), axis=-1, keepdims=True)
    lane_grp = ((ln - N_EXPERT_GROUPS) >> 3).astype(F32)
    in_sel = jnp.logical_and(
        jnp.logical_and(ln >= N_EXPERT_GROUPS, ln < N_EXPERT_GROUPS + N_EXPERTS),
        lane_grp == g_sel)
    el = jnp.where(in_sel, logits, ninf)
    l1 = jnp.max(el, axis=-1, keepdims=True)
    i1 = jnp.min(jnp.where(el == l1, lnf, float(ROUTER_LANES)), axis=-1, keepdims=True)
    el2 = jnp.where(lnf == i1, ninf, el)
    l2 = jnp.max(el2, axis=-1, keepdims=True)
    i2 = jnp.min(jnp.where(el2 == l2, lnf, float(ROUTER_LANES)), axis=-1, keepdims=True)
    tt = jnp.exp(l2 - l1)
    w1 = 1.0 / (1.0 + tt)
    comb1 = p_grp * w1
    comb2 = p_grp * (tt * w1)
    hit1 = lnf == i1
    hit2 = lnf == i2
    onehot = jnp.where(jnp.logical_or(hit1, hit2), 1.0, 0.0)
    rr = lax.broadcasted_iota(I32, (tq, tq), 0)
    cc = lax.broadcasted_iota(I32, (tq, tq), 1)
    lower = jnp.where(rr > cc, 1.0, 0.0).astype(BF16)
    prior = jnp.dot(lower, onehot.astype(BF16), preferred_element_type=F32) + carry[...]
    rank1 = jnp.sum(jnp.where(hit1, prior, 0.0), axis=-1, keepdims=True)
    rank2 = jnp.sum(jnp.where(hit2, prior, 0.0), axis=-1, keepdims=True)
    carry[...] = carry[...] + jnp.sum(onehot, axis=0, keepdims=True)
    cnt_ref[...] = carry[...].astype(I32)
    eid1 = i1 - float(N_EXPERT_GROUPS)
    eid2 = i2 - float(N_EXPERT_GROUPS)
    meta = jnp.where(ln == 0, eid1, jnp.where(ln == 1, eid2,
                     jnp.where(ln == 2, rank1, jnp.where(ln == 3, rank2, 0.0))))
    mi_ref[...] = meta.T[0:META_ROWS, :].astype(I32)
    mf_ref[...] = jnp.where(ln == 0, comb1, jnp.where(ln == 1, comb2, 0.0))


def _mix(sinks, proj, x2, mod3, w_pool, pool_scale, w_out, g2, w_r, b_r, bsz, seq):
    t, d = x2.shape
    n_in = proj.shape[1]
    per_b = seq // MIX_TQ
    q_per_win = MIX_TQ // WINDOW
    q_per_halo = MIX_TQ // MAX_POOL_WINDOW
    slabs = d // (2 * LANES)

    def row(b, j):
        return b * per_b + j

    out_shapes = (
        jax.ShapeDtypeStruct((t, d), F32),
        jax.ShapeDtypeStruct((t * slabs, LANES), U32),
        jax.ShapeDtypeStruct((META_ROWS, t), I32),
        jax.ShapeDtypeStruct((t, ROUTER_LANES), F32),
        jax.ShapeDtypeStruct((1, ROUTER_LANES), I32),
    )
    return pl.pallas_call(
        _mix_kernel,
        out_shape=out_shapes,
        grid=(bsz, per_b),
        in_specs=[
            pl.BlockSpec(memory_space=pltpu.SMEM),
            pl.BlockSpec((MIX_TQ, n_in), lambda b, j: (row(b, j), 0)),
            pl.BlockSpec((WINDOW, 2 * D_KV),
                         lambda b, j: (jnp.maximum(row(b, j) * q_per_win - 1, 0),
                                       (D_POOL + D_ATTN) // (2 * D_KV))),
            pl.BlockSpec((MAX_POOL_WINDOW, D_POOL),
                         lambda b, j: (jnp.maximum(row(b, j) * q_per_halo - 1, 0), 0)),
            pl.BlockSpec((MIX_TQ, d), lambda b, j: (row(b, j), 0)),
            pl.BlockSpec((1, N_MOD, d), lambda b, j: (b, 0, 0)),
            pl.BlockSpec(w_pool.shape, lambda b, j: (0, 0, 0)),
            pl.BlockSpec((1, D_POOL), lambda b, j: (0, 0)),
            pl.BlockSpec((d, d), lambda b, j: (0, 0)),
            pl.BlockSpec((1, d), lambda b, j: (0, 0)),
            pl.BlockSpec((d, ROUTER_LANES), lambda b, j: (0, 0)),
            pl.BlockSpec((1, ROUTER_LANES), lambda b, j: (0, 0)),
        ],
        out_specs=(
            pl.BlockSpec((MIX_TQ, d), lambda b, j: (row(b, j), 0)),
            pl.BlockSpec((MIX_TQ * slabs, LANES), lambda b, j: (row(b, j), 0)),
            pl.BlockSpec((META_ROWS, MIX_TQ), lambda b, j: (0, row(b, j))),
            pl.BlockSpec((MIX_TQ, ROUTER_LANES), lambda b, j: (row(b, j), 0)),
            pl.BlockSpec((1, ROUTER_LANES), lambda b, j: (0, 0)),
        ),
        scratch_shapes=[
            pltpu.VMEM((SUBLANES + MAX_POOL_WINDOW + MIX_TQ, D_POOL), F32),
            pltpu.VMEM((SUBLANES + MAX_POOL_WINDOW + MIX_TQ, POOL_GROUP), F32),
            pltpu.VMEM((SUBLANES + MAX_POOL_WINDOW + MIX_TQ, POOL_GROUP), F32),
            pltpu.VMEM((MIX_TQ, d), BF16),
            pltpu.VMEM((1, ROUTER_LANES), F32),
        ],
        compiler_params=pltpu.CompilerParams(
            dimension_semantics=("arbitrary", "arbitrary"), vmem_limit_bytes=VMEM_LIMIT),
        name="mix",
    )(sinks, proj, proj, proj, x2, mod3, w_pool, pool_scale, w_out, g2, w_r, b_r)


def _dest_kernel(pstart_ref, mi_ref, o_ref):
    e1 = mi_ref[0:1, :]
    e2 = mi_ref[1:2, :]
    base1 = jnp.zeros_like(e1)
    base2 = jnp.zeros_like(e2)
    for e in range(N_EXPERTS):
        start = pstart_ref[e]
        base1 = jnp.where(e1 == e, start, base1)
        base2 = jnp.where(e2 == e, start, base2)
    fields = lax.broadcasted_iota(I32, o_ref.shape, 0)
    o_ref[...] = jnp.where(fields == 0, base1 + mi_ref[2:3, :],
                           jnp.where(fields == 1, base2 + mi_ref[3:4, :], 0))


def _dest(pstarts, mi):
    return pl.pallas_call(
        _dest_kernel,
        out_shape=jax.ShapeDtypeStruct(mi.shape, I32),
        in_specs=[pl.BlockSpec(memory_space=pltpu.SMEM),
                  pl.BlockSpec(mi.shape, lambda: (0, 0))],
        out_specs=pl.BlockSpec(mi.shape, lambda: (0, 0)),
        name="dest",
    )(pstarts, mi)


def _dispatch_kernel(pstart_ref, count_ref, d1_ref, d2_ref, h2_ref, xr_hbm, zbuf, sem, zsem):
    tc = d1_ref.shape[0]
    step = pl.program_id(0)
    nblk = xr_hbm.shape[0] // MOE_BLK
    last = N_EXPERTS - 1
    n_used = (pstart_ref[last] + count_ref[last] + MOE_BLK - 1) // MOE_BLK

    def pad_copies(e):
        cnt = count_ref[e]
        npad = (-cnt) & (MOE_BLK - 1)
        off = pstart_ref[e] + cnt
        out = []
        for piece in PAD_PIECES:
            out.append((npad & piece, pltpu.make_async_copy(
                zbuf.at[pl.ds(0, piece)], xr_hbm.at[pl.ds(off, piece)], zsem)))
            off = off + (npad & piece)
        return out

    def tail_copy(blk):
        return pltpu.make_async_copy(zbuf, xr_hbm.at[pl.ds(blk * MOE_BLK, MOE_BLK)], zsem)

    def zero_fill(start):
        def per_expert(e, carry):
            for flag, cp in pad_copies(e):
                pl.when(flag != 0)(cp.start if start else cp.wait)
            return carry

        def per_tail(blk, carry):
            cp = tail_copy(blk)
            cp.start() if start else cp.wait()
            return carry

        lax.fori_loop(0, N_EXPERTS, per_expert, 0)
        lax.fori_loop(n_used, nblk, per_tail, 0)

    @pl.when(step == 0)
    def _():
        zbuf[...] = jnp.zeros_like(zbuf)
        zero_fill(True)

    def copies(tl):
        src = h2_ref.at[tl]
        return (pltpu.make_async_copy(src, xr_hbm.at[d1_ref[tl]], sem),
                pltpu.make_async_copy(src, xr_hbm.at[d2_ref[tl]], sem))

    def issue(i, carry):
        for u in range(DISPATCH_UNROLL):
            for k, cp in enumerate(copies(i * DISPATCH_UNROLL + u)):
                cp.start(priority=(2 * u + k) % DMA_THREADS)
        return carry

    def drain(i, carry):
        for u in range(DMA_UNROLL):
            for cp in copies(i * DMA_UNROLL + u):
                cp.wait()
        return carry

    lax.fori_loop(0, tc // DISPATCH_UNROLL, issue, 0)
    lax.fori_loop(0, tc // DMA_UNROLL, drain, 0)

    @pl.when(step == 0)
    def _():
        zero_fill(False)


def _dispatch(pstarts, counts, d1, d2, h2r, n_rows):
    t = d1.shape[0]
    tc = min(DISPATCH_TC, t)
    smem_blk = pl.BlockSpec((tc,), lambda i: (i,), memory_space=pltpu.SMEM)
    return pl.pallas_call(
        _dispatch_kernel,
        out_shape=jax.ShapeDtypeStruct((n_rows,) + h2r.shape[1:], h2r.dtype),
        grid=(t // tc,),
        in_specs=[
            pl.BlockSpec(memory_space=pltpu.SMEM),
            pl.BlockSpec(memory_space=pltpu.SMEM),
            smem_blk, smem_blk,
            pl.BlockSpec((tc,) + h2r.shape[1:], lambda i: (i, 0, 0)),
        ],
        out_specs=pl.BlockSpec(memory_space=pl.ANY),
        scratch_shapes=[pltpu.VMEM((MOE_BLK,) + h2r.shape[1:], h2r.dtype),
                        pltpu.SemaphoreType.DMA(()), pltpu.SemaphoreType.DMA(())],
        compiler_params=pltpu.CompilerParams(
            dimension_semantics=("arbitrary",), has_side_effects=True,
            vmem_limit_bytes=VMEM_LIMIT),
        name="dispatch",
    )(pstarts, counts, d1, d2, h2r)


def _experts_kernel(blk_e_ref, blk_row_ref, blk_valid_ref, blk_first_ref, blk_slot_ref,
                    *rest):
    ahead_refs = rest[:WEIGHT_SLOTS - 1]
    (xr_ref, wg_hbm, wu_hbm, wd_hbm, yr_ref,
     wg_f32, wu_f32, wd_f32, wg_bf, wu_bf, wd_bf, xb, wsems) = rest[WEIGHT_SLOTS - 1:]
    i = pl.program_id(0)
    nvalid = blk_valid_ref[i]
    slot = blk_slot_ref[i]

    def weight_copies(expert, s):
        return [pltpu.make_async_copy(src.at[expert], dst.at[s], wsems.at[s])
                for src, dst in ((wg_hbm, wg_f32), (wu_hbm, wu_f32), (wd_hbm, wd_f32))]

    def request(expert, k):
        s = slot + k
        s = jnp.where(s >= WEIGHT_SLOTS, s - WEIGHT_SLOTS, s)

        @pl.when(expert >= 0)
        def _():
            for cp in weight_copies(expert, s):
                cp.start()

    @pl.when(i == 0)
    def _():
        request(blk_e_ref[0], 0)
        for k in range(1, WEIGHT_SLOTS - 1):
            request(ahead_refs[k - 1][0], k)

    @pl.when(blk_first_ref[i] == 1)
    def _():
        request(ahead_refs[WEIGHT_SLOTS - 2][i], WEIGHT_SLOTS - 1)
        for cp in weight_copies(blk_e_ref[i], slot):
            cp.wait()
        wg_bf[...] = wg_f32[slot].astype(BF16)
        wu_bf[...] = wu_f32[slot].astype(BF16)
        wd_bf[...] = wd_f32[slot].astype(BF16)

    @pl.when(nvalid > 0)
    def _():
        rows = xb.shape[0]
        for sidx, slab in enumerate(_load_slabs(xr_ref, rows)):
            xb[:, sidx * LANES:(sidx + 1) * LANES] = slab.astype(BF16)
        xv = xb[...]
        gate = jnp.dot(xv, wg_bf[...], preferred_element_type=F32)
        up = jnp.dot(xv, wu_bf[...], preferred_element_type=F32)
        act = (_silu(gate) * up).astype(BF16)
        _store_slabs(yr_ref, jnp.dot(act, wd_bf[...], preferred_element_type=F32))

    @pl.when(nvalid == 0)
    def _():
        _store_slabs(yr_ref, jnp.zeros((xb.shape[0], xb.shape[1]), F32))


def _experts(schedule, xr, w_gate, w_up, w_down):
    n_rows, slabs, _ = xr.shape
    d = 2 * slabs * LANES
    nblk = n_rows // MOE_BLK
    grid_spec = pltpu.PrefetchScalarGridSpec(
        num_scalar_prefetch=len(schedule),
        grid=(nblk,),
        in_specs=[
            pl.BlockSpec((MOE_BLK * slabs, LANES), lambda i, *sched: (sched[1][i], 0)),
            pl.BlockSpec(memory_space=pl.ANY),
            pl.BlockSpec(memory_space=pl.ANY),
            pl.BlockSpec(memory_space=pl.ANY),
        ],
        out_specs=pl.BlockSpec((MOE_BLK * slabs, LANES), lambda i, *sched: (i, 0)),
        scratch_shapes=[
            pltpu.VMEM((WEIGHT_SLOTS, d, D_EXPERT), F32),
            pltpu.VMEM((WEIGHT_SLOTS, d, D_EXPERT), F32),
            pltpu.VMEM((WEIGHT_SLOTS, D_EXPERT, d), F32),
            pltpu.VMEM((d, D_EXPERT), BF16),
            pltpu.VMEM((d, D_EXPERT), BF16),
            pltpu.VMEM((D_EXPERT, d), BF16),
            pltpu.VMEM((MOE_BLK, d), BF16),
            pltpu.SemaphoreType.DMA((WEIGHT_SLOTS,)),
        ],
    )
    yr = pl.pallas_call(
        _experts_kernel,
        out_shape=jax.ShapeDtypeStruct((n_rows * slabs, LANES), U32),
        grid_spec=grid_spec,
        compiler_params=pltpu.CompilerParams(
            dimension_semantics=("arbitrary",), vmem_limit_bytes=VMEM_LIMIT),
        name="experts",
    )(*schedule, xr.reshape(n_rows * slabs, LANES), w_gate, w_up, w_down)
    return yr.reshape(n_rows, slabs, LANES)


def _combine_kernel(d1_ref, d2_ref, yr_hbm, x1_ref, mf_ref, mod_ref, gf_ref, o_ref,
                    ya0, yb0, ya1, yb1, sems):
    i = pl.program_id(0)
    n = pl.num_programs(0)
    tk = x1_ref.shape[0]
    bufs = ((ya0, yb0), (ya1, yb1))

    slabs = yr_hbm.shape[1]

    def copies(tile, slot, tl):
        tok = tile * tk + tl
        ya, yb = bufs[slot]
        dst = pl.ds(pl.multiple_of(tl * slabs, slabs), slabs)
        return (pltpu.make_async_copy(yr_hbm.at[d1_ref[tok]], ya.at[dst], sems.at[slot]),
                pltpu.make_async_copy(yr_hbm.at[d2_ref[tok]], yb.at[dst], sems.at[slot]))

    def issue(tile, slot):
        def body(k, carry):
            for u in range(DMA_UNROLL):
                for c, cp in enumerate(copies(tile, slot, k * DMA_UNROLL + u)):
                    cp.start(priority=(2 * u + c) % DMA_THREADS)
            return carry
        lax.fori_loop(0, tk // DMA_UNROLL, body, 0)

    def drain(tile, slot):
        def body(k, carry):
            for u in range(DMA_UNROLL):
                for cp in copies(tile, slot, k * DMA_UNROLL + u):
                    cp.wait()
            return carry
        lax.fori_loop(0, tk // DMA_UNROLL, body, 0)

    def step(slot):
        @pl.when(i == 0)
        def _():
            issue(i, slot)

        drain(i, slot)
        nxt = jnp.minimum(i + 1, n - 1)
        for tl in range(tk):
            for c, cp in enumerate(copies(nxt, 1 - slot, tl)):
                cp.start(priority=(2 * tl + c) % DMA_THREADS)
        ya, yb = bufs[slot]
        c1 = mf_ref[:, 0:1]
        c2 = mf_ref[:, 1:2]
        moe = jnp.concatenate(
            [c1 * sa + c2 * sb for sa, sb in zip(_load_slabs(ya, tk), _load_slabs(yb, tk))], axis=1)
        xo = x1_ref[...] + mod_ref[0, 5:6, :] * moe
        ms = jnp.mean(xo * xo, axis=-1, keepdims=True)
        o_ref[...] = (xo * lax.rsqrt(ms + RMS_EPS)) * gf_ref[...]

        @pl.when(i == n - 1)
        def _():
            drain(nxt, 1 - slot)

    @pl.when(i % 2 == 0)
    def _():
        step(0)

    @pl.when(i % 2 == 1)
    def _():
        step(1)


def _combine(d1, d2, yr, x1, mf, mod3, g_final, seq):
    t, d = x1.shape
    slabs = yr.shape[1]
    per_b = seq // COMBINE_TK
    grid_spec = pltpu.PrefetchScalarGridSpec(
        num_scalar_prefetch=2,
        grid=(t // COMBINE_TK,),
        in_specs=[
            pl.BlockSpec(memory_space=pl.ANY),
            pl.BlockSpec((COMBINE_TK, d), lambda i, a, b: (i, 0)),
            pl.BlockSpec((COMBINE_TK, ROUTER_LANES), lambda i, a, b: (i, 0)),
            pl.BlockSpec((1, N_MOD, d), lambda i, a, b: (i // per_b, 0, 0)),
            pl.BlockSpec((1, d), lambda i, a, b: (0, 0)),
        ],
        out_specs=pl.BlockSpec((COMBINE_TK, d), lambda i, a, b: (i, 0)),
        scratch_shapes=[pltpu.VMEM((COMBINE_TK * slabs, LANES), U32) for _ in range(4)]
        + [pltpu.SemaphoreType.DMA((2,))],
    )
    return pl.pallas_call(
        _combine_kernel,
        out_shape=jax.ShapeDtypeStruct((t, d), F32),
        grid_spec=grid_spec,
        compiler_params=pltpu.CompilerParams(
            dimension_semantics=("arbitrary",), vmem_limit_bytes=VMEM_LIMIT),
        name="combine",
    )(d1, d2, yr, x1, mf, mod3, g_final)


def _block_schedule(counts, n_rows):
    nblk = n_rows // MOE_BLK
    eidx = jnp.arange(N_EXPERTS, dtype=I32)
    bidx = jnp.arange(nblk, dtype=I32)
    nblocks = (counts + MOE_BLK - 1) // MOE_BLK
    bends = jnp.cumsum(nblocks)
    bstarts = bends - nblocks
    n_used = bends[-1]
    used = bidx < n_used
    blk_row = jnp.minimum(bidx, jnp.maximum(n_used - 1, 0))
    owner = jnp.logical_and(blk_row[:, None] >= bstarts[None, :], blk_row[:, None] < bends[None, :])

    def pick(per_expert):
        return jnp.sum(jnp.where(owner, per_expert[None, :], 0), axis=1).astype(I32)

    blk_e = pick(eidx)
    blk_valid = jnp.where(
        used, jnp.clip(pick(counts) - (bidx - pick(bstarts)) * MOE_BLK, 0, MOE_BLK), 0).astype(I32)
    blk_first = jnp.logical_and(used, bidx == pick(bstarts)).astype(I32)
    nonempty = counts > 0
    ordinal = jnp.cumsum(nonempty.astype(I32)) - 1
    blk_slot = pick(ordinal) % WEIGHT_SLOTS

    def ahead(k):
        match = jnp.logical_and(nonempty[None, :], ordinal[None, :] == ordinal[:, None] + k)
        found = jnp.sum(jnp.where(match, eidx[None, :], 0), axis=1)
        return pick(jnp.where(ordinal + k <= ordinal[-1], found, -1))

    blk_ahead = tuple(ahead(k) for k in range(1, WEIGHT_SLOTS))
    pstarts = (bstarts * MOE_BLK).astype(I32)
    return pstarts, (blk_e, blk_row, blk_valid, blk_first, blk_slot) + blk_ahead


def kernel(x, c, w_ada, b_ada, g_norm1, w_in, w_pool, pool_scale, attn_sinks, w_out,
           g_norm2, w_router_group, b_router_group, w_router_expert, b_router_expert,
           w_gate, w_up, w_down, g_final):
    bsz, seq, d = x.shape
    t = bsz * seq
    assert w_ada.shape[0] == 1, "single-layer model: the combine step applies the final norm"
    assert seq % INPROJ_TM == 0 and seq % MIX_TQ == 0 and seq % COMBINE_TK == 0, seq
    assert t % min(DISPATCH_TC, t) == 0 and d % (2 * LANES) == 0, (t, d)
    x2 = x.reshape(t, d)
    for l in range(1):
        mod3 = _ada(c, w_ada[l], b_ada[l]).reshape(bsz, N_MOD, d)
        proj = _inproj(x2, mod3, g_norm1[l].reshape(1, d), w_in[l].astype(BF16), seq)
        pad = ROUTER_LANES - N_EXPERT_GROUPS - N_EXPERTS
        w_r = jnp.concatenate(
            [w_router_group[l], w_router_expert[l], jnp.zeros((d, pad), F32)], axis=1).astype(BF16)
        b_r = jnp.concatenate(
            [b_router_group[l], b_router_expert[l], jnp.zeros((pad,), F32)]).reshape(1, ROUTER_LANES)
        x1, h2r, mi, mf, cnt = _mix(
            attn_sinks[l], proj, x2, mod3, w_pool[l].astype(BF16),
            pool_scale[l].reshape(1, D_POOL), w_out[l].astype(BF16),
            g_norm2[l].reshape(1, d), w_r, b_r, bsz, seq)
        n_rows = 2 * t + N_EXPERTS * MOE_BLK
        counts = cnt[0, N_EXPERT_GROUPS:N_EXPERT_GROUPS + N_EXPERTS]
        pstarts, schedule = _block_schedule(counts, n_rows)
        dest = _dest(pstarts, mi)
        d1, d2 = dest[0], dest[1]
        xr = _dispatch(pstarts, counts, d1, d2, h2r.reshape(t, d // (2 * LANES), LANES), n_rows)
        yr = _experts(schedule, xr, w_gate[l], w_up[l], w_down[l])
        x2 = _combine(d1, d2, yr, x1, mf, mod3, g_final.reshape(1, d), seq)
    return x2.reshape(bsz, seq, d)
```

```python
import functools

import jax
import jax.numpy as jnp
from jax import lax
from jax.experimental import pallas as pl
from jax.experimental.pallas import tpu as pltpu

F32 = jnp.float32
BF16 = jnp.bfloat16
I32 = jnp.int32
U32 = jnp.uint32

D_POOL = 1024
POOL_WINDOWS = (2, 4, 8, 16)
POOL_GROUP = 256
MAX_POOL_WINDOW = 16
HEAD_DIM = 64
N_KV_HEADS = 2
GQA_GROUP = 8
WINDOW = 128
D_ATTN = 1024
D_KV = 128
N_EXPERT_GROUPS = 8
EXPERTS_PER_GROUP = 8
N_EXPERTS = 64
D_EXPERT = 512
N_MOD = 6
RMS_EPS = 1e-6
NEG_INF = -1e30

LANES = 128
SUBLANES = 8
HEADS_PER_VREG = LANES // HEAD_DIM
PAIRS_PER_KV = GQA_GROUP // HEADS_PER_VREG

ADA_TN = 1024
INPROJ_TM = 1024
MIX_TQ = 512
MOE_BLK = 256
WEIGHT_SLOTS = 2
EXPERT_STEP_BLOCKS = 2
PAD_PIECES = tuple(MOE_BLK >> (k + 1) for k in range(MOE_BLK.bit_length() - 1))
DISPATCH_TC = 1024
COMBINE_TK = 256
DMA_UNROLL = 8
DMA_THREADS = 2
DISPATCH_UNROLL = 64
ROUTER_LANES = 128
META_ROWS = 8
VMEM_LIMIT = 56 * 1024 * 1024


def _silu(v):
    return v * jax.nn.sigmoid(v)


def _load_slabs(ref, rows):
    slabs = ref.shape[0] // rows
    words = [ref[pl.ds(s, rows, stride=slabs), :] for s in range(slabs)]
    return [pltpu.unpack_elementwise(w, index=half, packed_dtype=BF16, unpacked_dtype=F32)
            for half in range(2) for w in words]


def _store_slabs(ref, val):
    rows, d = val.shape
    slabs = ref.shape[0] // rows
    for s in range(slabs):
        lo = val[:, s * LANES:(s + 1) * LANES]
        hi = val[:, d // 2 + s * LANES:d // 2 + (s + 1) * LANES]
        ref[pl.ds(s, rows, stride=slabs), :] = pltpu.pack_elementwise([lo, hi], packed_dtype=BF16)


def _ada_kernel(c_ref, w_ref, b_ref, o_ref):
    ca = _silu(c_ref[...])
    o_ref[...] = jnp.dot(ca.astype(BF16), w_ref[...].astype(BF16),
                         preferred_element_type=F32) + b_ref[...]


def _ada(c, w, b):
    bsz, d = c.shape
    n = w.shape[1]
    return pl.pallas_call(
        _ada_kernel,
        out_shape=jax.ShapeDtypeStruct((bsz, n), F32),
        grid=(n // ADA_TN,),
        in_specs=[
            pl.BlockSpec((bsz, d), lambda i: (0, 0)),
            pl.BlockSpec((d, ADA_TN), lambda i: (0, i)),
            pl.BlockSpec((1, ADA_TN), lambda i: (0, i)),
        ],
        out_specs=pl.BlockSpec((bsz, ADA_TN), lambda i: (0, i)),
        compiler_params=pltpu.CompilerParams(
            dimension_semantics=("arbitrary",), vmem_limit_bytes=VMEM_LIMIT),
        name="ada",
    )(c, w, b.reshape(1, n))


def _norm_mod(x, g, shift, scale):
    ms = jnp.mean(x * x, axis=-1, keepdims=True)
    return (x * lax.rsqrt(ms + RMS_EPS)) * (g * (1.0 + scale)) + shift


def _inproj_kernel(x_ref, mod_ref, g_ref, w_ref, o_ref):
    h = _norm_mod(x_ref[...], g_ref[...], mod_ref[0, 0:1, :], mod_ref[0, 1:2, :])
    o_ref[...] = jnp.dot(h.astype(BF16), w_ref[...],
                         preferred_element_type=F32).astype(o_ref.dtype)


def _inproj(x2, mod3, g1, w_in, seq):
    t, d = x2.shape
    n = w_in.shape[1]
    per_b = seq // INPROJ_TM
    return pl.pallas_call(
        _inproj_kernel,
        out_shape=jax.ShapeDtypeStruct((t, n), BF16),
        grid=(t // INPROJ_TM,),
        in_specs=[
            pl.BlockSpec((INPROJ_TM, d), lambda i: (i, 0)),
            pl.BlockSpec((1, N_MOD, d), lambda i: (i // per_b, 0, 0)),
            pl.BlockSpec((1, d), lambda i: (0, 0)),
            pl.BlockSpec((d, n), lambda i: (0, 0)),
        ],
        out_specs=pl.BlockSpec((INPROJ_TM, n), lambda i: (i, 0)),
        compiler_params=pltpu.CompilerParams(
            dimension_semantics=("arbitrary",), vmem_limit_bytes=VMEM_LIMIT),
        name="inproj",
    )(x2, mod3, g1, w_in)


def _mix_kernel(sinks_ref, proj_ref, kvp_ref, up_ref, x_ref, mod_ref, wpool_ref,
                pscale_ref, wout_ref, g2_ref, wr_ref, br_ref,
                x1_ref, h2_ref, mi_ref, mf_ref, cnt_ref,
                ubuf, lvl_a, lvl_b, mixbuf, carry):
    b = pl.program_id(0)
    j = pl.program_id(1)
    tq = x_ref.shape[0]
    seq_start = j == 0

    @pl.when(jnp.logical_and(b == 0, j == 0))
    def _():
        carry[...] = jnp.zeros_like(carry)

    top = SUBLANES
    data = top + MAX_POOL_WINDOW
    ext = MAX_POOL_WINDOW + tq
    halo = up_ref[...].astype(F32)
    ubuf[0:top, :] = jnp.zeros((top, D_POOL), F32)
    ubuf[top:data, :] = jnp.where(seq_start, 0.0, halo)
    ubuf[data:data + tq, :] = proj_ref[:, 0:D_POOL].astype(F32)
    lvl_a[0:top, :] = jnp.zeros((top, POOL_GROUP), F32)
    lvl_b[0:top, :] = jnp.zeros((top, POOL_GROUP), F32)
    pos = j * tq + lax.broadcasted_iota(I32, (tq, 1), 0)
    for gi, w in enumerate(POOL_WINDOWS):
        c0 = gi * POOL_GROUP
        u = ubuf[data:data + tq, c0:c0 + POOL_GROUP]
        src, cols = ubuf, slice(c0, c0 + POOL_GROUP)
        span = 1
        while 2 * span < w:
            dst = lvl_b if src is lvl_a else lvl_a
            dst[top:top + ext, :] = src[top:top + ext, cols] + src[top - span:top - span + ext, cols]
            src, cols = dst, slice(0, POOL_GROUP)
            span *= 2
        acc = src[data:data + tq, cols] + src[data - span:data - span + tq, cols]
        cnt = jnp.minimum(pos + 1, w).astype(F32)
        delta = (acc / cnt - u).astype(BF16)
        yp = jnp.dot(delta, wpool_ref[gi], preferred_element_type=F32)
        yp = yp * pscale_ref[:, c0:c0 + POOL_GROUP]
        mixbuf[:, c0:c0 + POOL_GROUP] = yp.astype(BF16)

    lane = lax.broadcasted_iota(I32, (2 * WINDOW, LANES), 1)
    low = lane < HEAD_DIM
    qi = lax.broadcasted_iota(I32, (WINDOW, 2 * WINDOW), 0)
    kj = lax.broadcasted_iota(I32, (WINDOW, 2 * WINDOW), 1)
    dist = qi - kj + WINDOW
    band = jnp.logical_and(dist >= 0, dist < WINDOW)
    olane = lax.broadcasted_iota(I32, (WINDOW, LANES), 1)
    k_col = D_POOL + D_ATTN
    v_col = k_col + D_KV
    for blk in range(tq // WINDOW):
        r0 = blk * WINDOW
        if blk == 0:
            kcat = jnp.concatenate(
                [kvp_ref[:, 0:D_KV], proj_ref[0:WINDOW, k_col:k_col + D_KV]], axis=0)
            vcat = jnp.concatenate(
                [kvp_ref[:, D_KV:2 * D_KV], proj_ref[0:WINDOW, v_col:v_col + D_KV]], axis=0)
            mask = jnp.logical_and(band, kj >= jnp.where(seq_start, WINDOW, 0))
        else:
            kcat = proj_ref[r0 - WINDOW:r0 + WINDOW, k_col:k_col + D_KV]
            vcat = proj_ref[r0 - WINDOW:r0 + WINDOW, v_col:v_col + D_KV]
            mask = band
        kswap = pltpu.roll(kcat.astype(F32), HEAD_DIM, 1).astype(BF16)
        vswap = pltpu.roll(vcat.astype(F32), HEAD_DIM, 1).astype(BF16)
        zero = jnp.zeros_like(kcat)
        for g in range(N_KV_HEADS):
            ksrc_lo, ksrc_hi = (kcat, kswap) if g == 0 else (kswap, kcat)
            vsrc_lo, vsrc_hi = (vcat, vswap) if g == 0 else (vswap, vcat)
            kbd = jnp.concatenate([jnp.where(low, ksrc_lo, zero),
                                   jnp.where(low, zero, ksrc_hi)], axis=0)
            vbd = jnp.concatenate([jnp.where(low, vsrc_lo, zero),
                                   jnp.where(low, zero, vsrc_hi)], axis=0)
            q_col = D_POOL + g * GQA_GROUP * HEAD_DIM
            q = jnp.concatenate(
                [proj_ref[r0:r0 + WINDOW, q_col + p * LANES:q_col + (p + 1) * LANES]
                 for p in range(PAIRS_PER_KV)], axis=0)
            q = q * jnp.asarray(HEAD_DIM ** -0.5, BF16)
            s = lax.dot_general(q, kbd, (((1,), (1,)), ((), ())),
                                preferred_element_type=F32)
            probs = []
            rdens = []
            for p in range(PAIRS_PER_KV):
                row_p = []
                row_r = []
                for hh in range(HEADS_PER_VREG):
                    sink = sinks_ref[g * GQA_GROUP + p * HEADS_PER_VREG + hh]
                    sp = s[p * WINDOW:(p + 1) * WINDOW,
                           hh * 2 * WINDOW:(hh + 1) * 2 * WINDOW]
                    sp = jnp.where(mask, sp, NEG_INF)
                    m = jnp.maximum(jnp.max(sp, axis=-1, keepdims=True), sink)
                    e = jnp.exp(sp - m)
                    den = jnp.sum(e, axis=-1, keepdims=True) + jnp.exp(sink - m)
                    row_p.append(e.astype(BF16))
                    row_r.append(1.0 / den)
                probs.append(jnp.concatenate(row_p, axis=1))
                rdens.append(row_r)
            pmat = jnp.concatenate(probs, axis=0)
            o = jnp.dot(pmat, vbd, preferred_element_type=F32)
            for p in range(PAIRS_PER_KV):
                op = o[p * WINDOW:(p + 1) * WINDOW, :]
                norm = jnp.where(olane < HEAD_DIM, rdens[p][0], rdens[p][1])
                c0 = D_POOL + g * GQA_GROUP * HEAD_DIM + p * LANES
                mixbuf[r0:r0 + WINDOW, c0:c0 + LANES] = (op * norm).astype(BF16)

    y = jnp.dot(mixbuf[...], wout_ref[...], preferred_element_type=F32)
    x1 = x_ref[...] + mod_ref[0, 2:3, :] * y
    x1_ref[...] = x1

    h2 = _norm_mod(x1, g2_ref[...], mod_ref[0, 3:4, :], mod_ref[0, 4:5, :])
    _store_slabs(h2_ref, h2)

    logits = jnp.dot(h2.astype(BF16), wr_ref[...], preferred_element_type=F32) + br_ref[...]
    ln = lax.broadcasted_iota(I32, (tq, ROUTER_LANES), 1)
    lnf = ln.astype(F32)
    ninf = -jnp.inf
    is_g = ln < N_EXPERT_GROUPS
    gl = jnp.where(is_g, logits, ninf)
    gmax = jnp.max(gl, axis=-1, keepdims=True)
    g_sel = jnp.min(jnp.where(gl == gmax, lnf, float(ROUTER_LANES)), axis=-1, keepdims=True)
    p_grp = 1.0 / jnp.sum(jnp.where(is_g, jnp.exp(logits - gmax), 0.0), axis=-1, keepdims=True)
    lane_grp = ((ln - N_EXPERT_GROUPS) >> 3).astype(F32)
    in_sel = jnp.logical_and(
        jnp.logical_and(ln >= N_EXPERT_GROUPS, ln < N_EXPERT_GROUPS + N_EXPERTS),
        lane_grp == g_sel)
    el = jnp.where(in_sel, logits, ninf)
    l1 = jnp.max(el, axis=-1, keepdims=True)
    i1 = jnp.min(jnp.where(el == l1, lnf, float(ROUTER_LANES)), axis=-1, keepdims=True)
    el2 = jnp.where(lnf == i1, ninf, el)
    l2 = jnp.max(el2, axis=-1, keepdims=True)
    i2 = jnp.min(jnp.where(el2 == l2, lnf, float(ROUTER_LANES)), axis=-1, keepdims=True)
    tt = jnp.exp(l2 - l1)
    w1 = 1.0 / (1.0 + tt)
    comb1 = p_grp * w1
    comb2 = p_grp * (tt * w1)
    hit1 = lnf == i1
    hit2 = lnf == i2
    onehot = jnp.where(jnp.logical_or(hit1, hit2), 1.0, 0.0)
    rr = lax.broadcasted_iota(I32, (tq, tq), 0)
    cc = lax.broadcasted_iota(I32, (tq, tq), 1)
    lower = jnp.where(rr > cc, 1.0, 0.0).astype(BF16)
    prior = jnp.dot(lower, onehot.astype(BF16), preferred_element_type=F32) + carry[...]
    rank1 = jnp.sum(jnp.where(hit1, prior, 0.0), axis=-1, keepdims=True)
    rank2 = jnp.sum(jnp.where(hit2, prior, 0.0), axis=-1, keepdims=True)
    carry[...] = carry[...] + jnp.sum(onehot, axis=0, keepdims=True)
    cnt_ref[...] = carry[...].astype(I32)
    eid1 = i1 - float(N_EXPERT_GROUPS)
    eid2 = i2 - float(N_EXPERT_GROUPS)
    meta = jnp.where(ln == 0, eid1, jnp.where(ln == 1, eid2,
                     jnp.where(ln == 2, rank1, jnp.where(ln == 3, rank2, 0.0))))
    mi_ref[...] = meta.T[0:META_ROWS, :].astype(I32)
    mf_ref[...] = jnp.where(ln == 0, comb1, jnp.where(ln == 1, comb2, 0.0))


def _mix(sinks, proj, x2, mod3, w_pool, pool_scale, w_out, g2, w_r, b_r, bsz, seq):
    t, d = x2.shape
    n_in = proj.shape[1]
    per_b = seq // MIX_TQ
    q_per_win = MIX_TQ // WINDOW
    q_per_halo = MIX_TQ // MAX_POOL_WINDOW
    slabs = d // (2 * LANES)

    def row(b, j):
        return b * per_b + j

    out_shapes = (
        jax.ShapeDtypeStruct((t, d), F32),
        jax.ShapeDtypeStruct((t * slabs, LANES), U32),
        jax.ShapeDtypeStruct((META_ROWS, t), I32),
        jax.ShapeDtypeStruct((t, ROUTER_LANES), F32),
        jax.ShapeDtypeStruct((1, ROUTER_LANES), I32),
    )
    return pl.pallas_call(
        _mix_kernel,
        out_shape=out_shapes,
        grid=(bsz, per_b),
        in_specs=[
            pl.BlockSpec(memory_space=pltpu.SMEM),
            pl.BlockSpec((MIX_TQ, n_in), lambda b, j: (row(b, j), 0)),
            pl.BlockSpec((WINDOW, 2 * D_KV),
                         lambda b, j: (jnp.maximum(row(b, j) * q_per_win - 1, 0),
                                       (D_POOL + D_ATTN) // (2 * D_KV))),
            pl.BlockSpec((MAX_POOL_WINDOW, D_POOL),
                         lambda b, j: (jnp.maximum(row(b, j) * q_per_halo - 1, 0), 0)),
            pl.BlockSpec((MIX_TQ, d), lambda b, j: (row(b, j), 0)),
            pl.BlockSpec((1, N_MOD, d), lambda b, j: (b, 0, 0)),
            pl.BlockSpec(w_pool.shape, lambda b, j: (0, 0, 0)),
            pl.BlockSpec((1, D_POOL), lambda b, j: (0, 0)),
            pl.BlockSpec((d, d), lambda b, j: (0, 0)),
            pl.BlockSpec((1, d), lambda b, j: (0, 0)),
            pl.BlockSpec((d, ROUTER_LANES), lambda b, j: (0, 0)),
            pl.BlockSpec((1, ROUTER_LANES), lambda b, j: (0, 0)),
        ],
        out_specs=(
            pl.BlockSpec((MIX_TQ, d), lambda b, j: (row(b, j), 0)),
            pl.BlockSpec((MIX_TQ * slabs, LANES), lambda b, j: (row(b, j), 0)),
            pl.BlockSpec((META_ROWS, MIX_TQ), lambda b, j: (0, row(b, j))),
            pl.BlockSpec((MIX_TQ, ROUTER_LANES), lambda b, j: (row(b, j), 0)),
            pl.BlockSpec((1, ROUTER_LANES), lambda b, j: (0, 0)),
        ),
        scratch_shapes=[
            pltpu.VMEM((SUBLANES + MAX_POOL_WINDOW + MIX_TQ, D_POOL), F32),
            pltpu.VMEM((SUBLANES + MAX_POOL_WINDOW + MIX_TQ, POOL_GROUP), F32),
            pltpu.VMEM((SUBLANES + MAX_POOL_WINDOW + MIX_TQ, POOL_GROUP), F32),
            pltpu.VMEM((MIX_TQ, d), BF16),
            pltpu.VMEM((1, ROUTER_LANES), F32),
        ],
        compiler_params=pltpu.CompilerParams(
            dimension_semantics=("arbitrary", "arbitrary"), vmem_limit_bytes=VMEM_LIMIT),
        name="mix",
    )(sinks, proj, proj, proj, x2, mod3, w_pool, pool_scale, w_out, g2, w_r, b_r)


def _dest_kernel(pstart_ref, mi_ref, o_ref):
    e1 = mi_ref[0:1, :]
    e2 = mi_ref[1:2, :]
    base1 = jnp.zeros_like(e1)
    base2 = jnp.zeros_like(e2)
    for e in range(N_EXPERTS):
        start = pstart_ref[e]
        base1 = jnp.where(e1 == e, start, base1)
        base2 = jnp.where(e2 == e, start, base2)
    fields = lax.broadcasted_iota(I32, o_ref.shape, 0)
    o_ref[...] = jnp.where(fields == 0, base1 + mi_ref[2:3, :],
                           jnp.where(fields == 1, base2 + mi_ref[3:4, :], 0))


def _dest(pstarts, mi):
    return pl.pallas_call(
        _dest_kernel,
        out_shape=jax.ShapeDtypeStruct(mi.shape, I32),
        in_specs=[pl.BlockSpec(memory_space=pltpu.SMEM),
                  pl.BlockSpec(mi.shape, lambda: (0, 0))],
        out_specs=pl.BlockSpec(mi.shape, lambda: (0, 0)),
        name="dest",
    )(pstarts, mi)


def _dispatch_kernel(pstart_ref, count_ref, d1_ref, d2_ref, h2_ref, xr_hbm, zbuf, sem, zsem):
    tc = d1_ref.shape[0]
    step = pl.program_id(0)
    nblk = xr_hbm.shape[0] // MOE_BLK
    last = N_EXPERTS - 1
    n_used = (pstart_ref[last] + count_ref[last] + MOE_BLK - 1) // MOE_BLK

    def pad_copies(e):
        cnt = count_ref[e]
        npad = (-cnt) & (MOE_BLK - 1)
        off = pstart_ref[e] + cnt
        out = []
        for piece in PAD_PIECES:
            out.append((npad & piece, pltpu.make_async_copy(
                zbuf.at[pl.ds(0, piece)], xr_hbm.at[pl.ds(off, piece)], zsem)))
            off = off + (npad & piece)
        return out

    def tail_copy(blk):
        return pltpu.make_async_copy(zbuf, xr_hbm.at[pl.ds(blk * MOE_BLK, MOE_BLK)], zsem)

    def zero_fill(start):
        def per_expert(e, carry):
            for flag, cp in pad_copies(e):
                pl.when(flag != 0)(cp.start if start else cp.wait)
            return carry

        def per_tail(blk, carry):
            cp = tail_copy(blk)
            cp.start() if start else cp.wait()
            return carry

        lax.fori_loop(0, N_EXPERTS, per_expert, 0)
        lax.fori_loop(n_used, nblk, per_tail, 0)

    @pl.when(step == 0)
    def _():
        zbuf[...] = jnp.zeros_like(zbuf)
        zero_fill(True)

    def copies(tl):
        src = h2_ref.at[tl]
        return (pltpu.make_async_copy(src, xr_hbm.at[d1_ref[tl]], sem),
                pltpu.make_async_copy(src, xr_hbm.at[d2_ref[tl]], sem))

    def issue(i, carry):
        for u in range(DISPATCH_UNROLL):
            for k, cp in enumerate(copies(i * DISPATCH_UNROLL + u)):
                cp.start(priority=(2 * u + k) % DMA_THREADS)
        return carry

    def drain(i, carry):
        for u in range(DMA_UNROLL):
            for cp in copies(i * DMA_UNROLL + u):
                cp.wait()
        return carry

    lax.fori_loop(0, tc // DISPATCH_UNROLL, issue, 0)
    lax.fori_loop(0, tc // DMA_UNROLL, drain, 0)

    @pl.when(step == 0)
    def _():
        zero_fill(False)


def _dispatch(pstarts, counts, d1, d2, h2r, n_rows):
    t = d1.shape[0]
    tc = min(DISPATCH_TC, t)
    smem_blk = pl.BlockSpec((tc,), lambda i: (i,), memory_space=pltpu.SMEM)
    return pl.pallas_call(
        _dispatch_kernel,
        out_shape=jax.ShapeDtypeStruct((n_rows,) + h2r.shape[1:], h2r.dtype),
        grid=(t // tc,),
        in_specs=[
            pl.BlockSpec(memory_space=pltpu.SMEM),
            pl.BlockSpec(memory_space=pltpu.SMEM),
            smem_blk, smem_blk,
            pl.BlockSpec((tc,) + h2r.shape[1:], lambda i: (i, 0, 0)),
        ],
        out_specs=pl.BlockSpec(memory_space=pl.ANY),
        scratch_shapes=[pltpu.VMEM((MOE_BLK,) + h2r.shape[1:], h2r.dtype),
                        pltpu.SemaphoreType.DMA(()), pltpu.SemaphoreType.DMA(())],
        compiler_params=pltpu.CompilerParams(
            dimension_semantics=("arbitrary",), has_side_effects=True,
            vmem_limit_bytes=VMEM_LIMIT),
        name="dispatch",
    )(pstarts, counts, d1, d2, h2r)


def _experts_kernel(blk_e_ref, blk_row_ref, blk_valid_ref, blk_first_ref, blk_slot_ref,
                    *rest):
    ahead_refs = rest[:WEIGHT_SLOTS - 1]
    (xr_ref, wg_hbm, wu_hbm, wd_hbm, yr_ref,
     wg_f32, wu_f32, wd_f32, wg_bf, wu_bf, wd_bf, xb, wsems) = rest[WEIGHT_SLOTS - 1:]
    rows = xb.shape[0]
    words = xr_ref.shape[0] // EXPERT_STEP_BLOCKS

    def weight_copies(expert, s):
        return [pltpu.make_async_copy(src.at[expert], dst.at[s], wsems.at[s])
                for src, dst in ((wg_hbm, wg_f32), (wu_hbm, wu_f32), (wd_hbm, wd_f32))]

    def one_block(blk, x_ref, y_ref, very_first):
        nvalid = blk_valid_ref[blk]
        slot = blk_slot_ref[blk]

        def request(expert, k):
            s = slot + k
            s = jnp.where(s >= WEIGHT_SLOTS, s - WEIGHT_SLOTS, s)

            @pl.when(expert >= 0)
            def _():
                for cp in weight_copies(expert, s):
                    cp.start()

        if very_first is not None:
            @pl.when(very_first)
            def _():
                request(blk_e_ref[0], 0)
                for k in range(1, WEIGHT_SLOTS - 1):
                    request(ahead_refs[k - 1][0], k)

        @pl.when(blk_first_ref[blk] == 1)
        def _():
            request(ahead_refs[WEIGHT_SLOTS - 2][blk], WEIGHT_SLOTS - 1)
            for cp in weight_copies(blk_e_ref[blk], slot):
                cp.wait()
            wg_bf[...] = wg_f32[slot].astype(BF16)
            wu_bf[...] = wu_f32[slot].astype(BF16)
            wd_bf[...] = wd_f32[slot].astype(BF16)

        @pl.when(nvalid > 0)
        def _():
            for sidx, slab in enumerate(_load_slabs(x_ref, rows)):
                xb[:, sidx * LANES:(sidx + 1) * LANES] = slab.astype(BF16)
            xv = xb[...]
            gate = jnp.dot(xv, wg_bf[...], preferred_element_type=F32)
            up = jnp.dot(xv, wu_bf[...], preferred_element_type=F32)
            act = (_silu(gate) * up).astype(BF16)
            _store_slabs(y_ref, jnp.dot(act, wd_bf[...], preferred_element_type=F32))

        @pl.when(nvalid == 0)
        def _():
            _store_slabs(y_ref, jnp.zeros((rows, xb.shape[1]), F32))

    step = pl.program_id(0)
    for h in range(EXPERT_STEP_BLOCKS):
        part = pl.ds(h * words, words)
        one_block(step * EXPERT_STEP_BLOCKS + h, xr_ref.at[part], yr_ref.at[part],
                  step == 0 if h == 0 else None)


def _experts(schedule, xr, w_gate, w_up, w_down):
    n_rows, slabs, _ = xr.shape
    d = 2 * slabs * LANES
    nblk = n_rows // MOE_BLK
    per_step = EXPERT_STEP_BLOCKS
    assert nblk % per_step == 0, nblk
    grid_spec = pltpu.PrefetchScalarGridSpec(
        num_scalar_prefetch=len(schedule),
        grid=(nblk // per_step,),
        in_specs=[
            pl.BlockSpec((per_step * MOE_BLK * slabs, LANES),
                         lambda i, *sched: (sched[1][i * per_step] // per_step, 0)),
            pl.BlockSpec(memory_space=pl.ANY),
            pl.BlockSpec(memory_space=pl.ANY),
            pl.BlockSpec(memory_space=pl.ANY),
        ],
        out_specs=pl.BlockSpec((per_step * MOE_BLK * slabs, LANES), lambda i, *sched: (i, 0)),
        scratch_shapes=[
            pltpu.VMEM((WEIGHT_SLOTS, d, D_EXPERT), F32),
            pltpu.VMEM((WEIGHT_SLOTS, d, D_EXPERT), F32),
            pltpu.VMEM((WEIGHT_SLOTS, D_EXPERT, d), F32),
            pltpu.VMEM((d, D_EXPERT), BF16),
            pltpu.VMEM((d, D_EXPERT), BF16),
            pltpu.VMEM((D_EXPERT, d), BF16),
            pltpu.VMEM((MOE_BLK, d), BF16),
            pltpu.SemaphoreType.DMA((WEIGHT_SLOTS,)),
        ],
    )
    yr = pl.pallas_call(
        _experts_kernel,
        out_shape=jax.ShapeDtypeStruct((n_rows * slabs, LANES), U32),
        grid_spec=grid_spec,
        compiler_params=pltpu.CompilerParams(
            dimension_semantics=("arbitrary",), vmem_limit_bytes=VMEM_LIMIT),
        name="experts",
    )(*schedule, xr.reshape(n_rows * slabs, LANES), w_gate, w_up, w_down)
    return yr.reshape(n_rows, slabs, LANES)


def _combine_kernel(d1_ref, d2_ref, yr_hbm, x1_ref, mf_ref, mod_ref, gf_ref, o_ref,
                    ya0, yb0, ya1, yb1, sems):
    i = pl.program_id(0)
    n = pl.num_programs(0)
    tk = x1_ref.shape[0]
    bufs = ((ya0, yb0), (ya1, yb1))

    slabs = yr_hbm.shape[1]

    def copies(tile, slot, tl):
        tok = tile * tk + tl
        ya, yb = bufs[slot]
        dst = pl.ds(pl.multiple_of(tl * slabs, slabs), slabs)
        return (pltpu.make_async_copy(yr_hbm.at[d1_ref[tok]], ya.at[dst], sems.at[slot]),
                pltpu.make_async_copy(yr_hbm.at[d2_ref[tok]], yb.at[dst], sems.at[slot]))

    def issue(tile, slot):
        def body(k, carry):
            for u in range(DMA_UNROLL):
                for c, cp in enumerate(copies(tile, slot, k * DMA_UNROLL + u)):
                    cp.start(priority=(2 * u + c) % DMA_THREADS)
            return carry
        lax.fori_loop(0, tk // DMA_UNROLL, body, 0)

    def drain(tile, slot):
        def body(k, carry):
            for u in range(DMA_UNROLL):
                for cp in copies(tile, slot, k * DMA_UNROLL + u):
                    cp.wait()
            return carry
        lax.fori_loop(0, tk // DMA_UNROLL, body, 0)

    def step(slot):
        @pl.when(i == 0)
        def _():
            issue(i, slot)

        drain(i, slot)
        nxt = jnp.minimum(i + 1, n - 1)
        for tl in range(tk):
            for c, cp in enumerate(copies(nxt, 1 - slot, tl)):
                cp.start(priority=(2 * tl + c) % DMA_THREADS)
        ya, yb = bufs[slot]
        c1 = mf_ref[:, 0:1]
        c2 = mf_ref[:, 1:2]
        moe = jnp.concatenate(
            [c1 * sa + c2 * sb for sa, sb in zip(_load_slabs(ya, tk), _load_slabs(yb, tk))], axis=1)
        xo = x1_ref[...] + mod_ref[0, 5:6, :] * moe
        ms = jnp.mean(xo * xo, axis=-1, keepdims=True)
        o_ref[...] = (xo * lax.rsqrt(ms + RMS_EPS)) * gf_ref[...]

        @pl.when(i == n - 1)
        def _():
            drain(nxt, 1 - slot)

    @pl.when(i % 2 == 0)
    def _():
        step(0)

    @pl.when(i % 2 == 1)
    def _():
        step(1)


def _combine(d1, d2, yr, x1, mf, mod3, g_final, seq):
    t, d = x1.shape
    slabs = yr.shape[1]
    per_b = seq // COMBINE_TK
    grid_spec = pltpu.PrefetchScalarGridSpec(
        num_scalar_prefetch=2,
        grid=(t // COMBINE_TK,),
        in_specs=[
            pl.BlockSpec(memory_space=pl.ANY),
            pl.BlockSpec((COMBINE_TK, d), lambda i, a, b: (i, 0)),
            pl.BlockSpec((COMBINE_TK, ROUTER_LANES), lambda i, a, b: (i, 0)),
            pl.BlockSpec((1, N_MOD, d), lambda i, a, b: (i // per_b, 0, 0)),
            pl.BlockSpec((1, d), lambda i, a, b: (0, 0)),
        ],
        out_specs=pl.BlockSpec((COMBINE_TK, d), lambda i, a, b: (i, 0)),
        scratch_shapes=[pltpu.VMEM((COMBINE_TK * slabs, LANES), U32) for _ in range(4)]
        + [pltpu.SemaphoreType.DMA((2,))],
    )
    return pl.pallas_call(
        _combine_kernel,
        out_shape=jax.ShapeDtypeStruct((t, d), F32),
        grid_spec=grid_spec,
        compiler_params=pltpu.CompilerParams(
            dimension_semantics=("arbitrary",), vmem_limit_bytes=VMEM_LIMIT),
        name="combine",
    )(d1, d2, yr, x1, mf, mod3, g_final)


def _block_schedule(counts, n_rows):
    nblk = n_rows // MOE_BLK
    eidx = jnp.arange(N_EXPERTS, dtype=I32)
    bidx = jnp.arange(nblk, dtype=I32)
    nblocks = (counts + MOE_BLK - 1) // MOE_BLK
    bends = jnp.cumsum(nblocks)
    bstarts = bends - nblocks
    n_used = bends[-1]
    used = bidx < n_used
    blk_row = jnp.minimum(bidx, jnp.maximum(n_used - 1, 0))
    owner = jnp.logical_and(blk_row[:, None] >= bstarts[None, :], blk_row[:, None] < bends[None, :])

    def pick(per_expert):
        return jnp.sum(jnp.where(owner, per_expert[None, :], 0), axis=1).astype(I32)

    blk_e = pick(eidx)
    blk_valid = jnp.where(
        used, jnp.clip(pick(counts) - (bidx - pick(bstarts)) * MOE_BLK, 0, MOE_BLK), 0).astype(I32)
    blk_first = jnp.logical_and(used, bidx == pick(bstarts)).astype(I32)
    nonempty = counts > 0
    ordinal = jnp.cumsum(nonempty.astype(I32)) - 1
    blk_slot = pick(ordinal) % WEIGHT_SLOTS

    def ahead(k):
        match = jnp.logical_and(nonempty[None, :], ordinal[None, :] == ordinal[:, None] + k)
        found = jnp.sum(jnp.where(match, eidx[None, :], 0), axis=1)
        return pick(jnp.where(ordinal + k <= ordinal[-1], found, -1))

    blk_ahead = tuple(ahead(k) for k in range(1, WEIGHT_SLOTS))
    pstarts = (bstarts * MOE_BLK).astype(I32)
    return pstarts, (blk_e, blk_row, blk_valid, blk_first, blk_slot) + blk_ahead


def kernel(x, c, w_ada, b_ada, g_norm1, w_in, w_pool, pool_scale, attn_sinks, w_out,
           g_norm2, w_router_group, b_router_group, w_router_expert, b_router_expert,
           w_gate, w_up, w_down, g_final):
    bsz, seq, d = x.shape
    t = bsz * seq
    assert w_ada.shape[0] == 1, "single-layer model: the combine step applies the final norm"
    assert seq % INPROJ_TM == 0 and seq % MIX_TQ == 0 and seq % COMBINE_TK == 0, seq
    assert t % min(DISPATCH_TC, t) == 0 and d % (2 * LANES) == 0, (t, d)
    x2 = x.reshape(t, d)
    for l in range(1):
        mod3 = _ada(c, w_ada[l], b_ada[l]).reshape(bsz, N_MOD, d)
        proj = _inproj(x2, mod3, g_norm1[l].reshape(1, d), w_in[l].astype(BF16), seq)
        pad = ROUTER_LANES - N_EXPERT_GROUPS - N_EXPERTS
        w_r = jnp.concatenate(
            [w_router_group[l], w_router_expert[l], jnp.zeros((d, pad), F32)], axis=1).astype(BF16)
        b_r = jnp.concatenate(
            [b_router_group[l], b_router_expert[l], jnp.zeros((pad,), F32)]).reshape(1, ROUTER_LANES)
        x1, h2r, mi, mf, cnt = _mix(
            attn_sinks[l], proj, x2, mod3, w_pool[l].astype(BF16),
            pool_scale[l].reshape(1, D_POOL), w_out[l].astype(BF16),
            g_norm2[l].reshape(1, d), w_r, b_r, bsz, seq)
        n_rows = 2 * t + N_EXPERTS * MOE_BLK
        counts = cnt[0, N_EXPERT_GROUPS:N_EXPERT_GROUPS + N_EXPERTS]
        pstarts, schedule = _block_schedule(counts, n_rows)
        dest = _dest(pstarts, mi)
        d1, d2 = dest[0], dest[1]
        xr = _dispatch(pstarts, counts, d1, d2, h2r.reshape(t, d // (2 * LANES), LANES), n_rows)
        yr = _experts(schedule, xr, w_gate[l], w_up[l], w_down[l])
        x2 = _combine(d1, d2, yr, x1, mf, mod3, g_final.reshape(1, d), seq)
    return x2.reshape(bsz, seq, d)
```

```python
import functools

import jax
import jax.numpy as jnp
from jax import lax
from jax.experimental import pallas as pl
from jax.experimental.pallas import tpu as pltpu

F32 = jnp.float32
BF16 = jnp.bfloat16
I32 = jnp.int32
U32 = jnp.uint32

D_POOL = 1024
POOL_WINDOWS = (2, 4, 8, 16)
POOL_GROUP = 256
MAX_POOL_WINDOW = 16
HEAD_DIM = 64
N_KV_HEADS = 2
GQA_GROUP = 8
WINDOW = 128
D_ATTN = 1024
D_KV = 128
N_EXPERT_GROUPS = 8
EXPERTS_PER_GROUP = 8
N_EXPERTS = 64
D_EXPERT = 512
N_MOD = 6
RMS_EPS = 1e-6
NEG_INF = -1e30

LANES = 128
SUBLANES = 8
HEADS_PER_VREG = LANES // HEAD_DIM
PAIRS_PER_KV = GQA_GROUP // HEADS_PER_VREG

ADA_TN = 1024
INPROJ_TM = 1024
MIX_TQ = 512
MOE_BLK = 256
WEIGHT_SLOTS = 2
EXPERT_STEP_BLOCKS = 4
PAD_PIECES = tuple(MOE_BLK >> (k + 1) for k in range(MOE_BLK.bit_length() - 1))
DISPATCH_TC = 1024
COMBINE_TK = 256
DMA_UNROLL = 8
DMA_THREADS = 2
DISPATCH_UNROLL = 64
ROUTER_LANES = 128
META_ROWS = 8
VMEM_LIMIT = 56 * 1024 * 1024


def _silu(v):
    return v * jax.nn.sigmoid(v)


def _load_slabs(ref, rows):
    slabs = ref.shape[0] // rows
    words = [ref[pl.ds(s, rows, stride=slabs), :] for s in range(slabs)]
    return [pltpu.unpack_elementwise(w, index=half, packed_dtype=BF16, unpacked_dtype=F32)
            for half in range(2) for w in words]


def _store_slabs(ref, val):
    rows, d = val.shape
    slabs = ref.shape[0] // rows
    for s in range(slabs):
        lo = val[:, s * LANES:(s + 1) * LANES]
        hi = val[:, d // 2 + s * LANES:d // 2 + (s + 1) * LANES]
        ref[pl.ds(s, rows, stride=slabs), :] = pltpu.pack_elementwise([lo, hi], packed_dtype=BF16)


def _ada_kernel(c_ref, w_ref, b_ref, o_ref):
    ca = _silu(c_ref[...])
    o_ref[...] = jnp.dot(ca.astype(BF16), w_ref[...].astype(BF16),
                         preferred_element_type=F32) + b_ref[...]


def _ada(c, w, b):
    bsz, d = c.shape
    n = w.shape[1]
    return pl.pallas_call(
        _ada_kernel,
        out_shape=jax.ShapeDtypeStruct((bsz, n), F32),
        grid=(n // ADA_TN,),
        in_specs=[
            pl.BlockSpec((bsz, d), lambda i: (0, 0)),
            pl.BlockSpec((d, ADA_TN), lambda i: (0, i)),
            pl.BlockSpec((1, ADA_TN), lambda i: (0, i)),
        ],
        out_specs=pl.BlockSpec((bsz, ADA_TN), lambda i: (0, i)),
        compiler_params=pltpu.CompilerParams(
            dimension_semantics=("arbitrary",), vmem_limit_bytes=VMEM_LIMIT),
        name="ada",
    )(c, w, b.reshape(1, n))


def _norm_mod(x, g, shift, scale):
    ms = jnp.mean(x * x, axis=-1, keepdims=True)
    return (x * lax.rsqrt(ms + RMS_EPS)) * (g * (1.0 + scale)) + shift


def _inproj_kernel(x_ref, mod_ref, g_ref, w_ref, o_ref):
    h = _norm_mod(x_ref[...], g_ref[...], mod_ref[0, 0:1, :], mod_ref[0, 1:2, :])
    o_ref[...] = jnp.dot(h.astype(BF16), w_ref[...],
                         preferred_element_type=F32).astype(o_ref.dtype)


def _inproj(x2, mod3, g1, w_in, seq):
    t, d = x2.shape
    n = w_in.shape[1]
    per_b = seq // INPROJ_TM
    return pl.pallas_call(
        _inproj_kernel,
        out_shape=jax.ShapeDtypeStruct((t, n), BF16),
        grid=(t // INPROJ_TM,),
        in_specs=[
            pl.BlockSpec((INPROJ_TM, d), lambda i: (i, 0)),
            pl.BlockSpec((1, N_MOD, d), lambda i: (i // per_b, 0, 0)),
            pl.BlockSpec((1, d), lambda i: (0, 0)),
            pl.BlockSpec((d, n), lambda i: (0, 0)),
        ],
        out_specs=pl.BlockSpec((INPROJ_TM, n), lambda i: (i, 0)),
        compiler_params=pltpu.CompilerParams(
            dimension_semantics=("arbitrary",), vmem_limit_bytes=VMEM_LIMIT),
        name="inproj",
    )(x2, mod3, g1, w_in)


def _mix_kernel(sinks_ref, proj_ref, kvp_ref, up_ref, x_ref, mod_ref, wpool_ref,
                pscale_ref, wout_ref, g2_ref, wr_ref, br_ref,
                x1_ref, h2_ref, mi_ref, mf_ref, cnt_ref,
                ubuf, lvl_a, lvl_b, mixbuf, carry):
    b = pl.program_id(0)
    j = pl.program_id(1)
    tq = x_ref.shape[0]
    seq_start = j == 0

    @pl.when(jnp.logical_and(b == 0, j == 0))
    def _():
        carry[...] = jnp.zeros_like(carry)

    top = SUBLANES
    data = top + MAX_POOL_WINDOW
    ext = MAX_POOL_WINDOW + tq
    halo = up_ref[...].astype(F32)
    ubuf[0:top, :] = jnp.zeros((top, D_POOL), F32)
    ubuf[top:data, :] = jnp.where(seq_start, 0.0, halo)
    ubuf[data:data + tq, :] = proj_ref[:, 0:D_POOL].astype(F32)
    lvl_a[0:top, :] = jnp.zeros((top, POOL_GROUP), F32)
    lvl_b[0:top, :] = jnp.zeros((top, POOL_GROUP), F32)
    pos = j * tq + lax.broadcasted_iota(I32, (tq, 1), 0)
    for gi, w in enumerate(POOL_WINDOWS):
        c0 = gi * POOL_GROUP
        u = ubuf[data:data + tq, c0:c0 + POOL_GROUP]
        src, cols = ubuf, slice(c0, c0 + POOL_GROUP)
        span = 1
        while 2 * span < w:
            dst = lvl_b if src is lvl_a else lvl_a
            dst[top:top + ext, :] = src[top:top + ext, cols] + src[top - span:top - span + ext, cols]
            src, cols = dst, slice(0, POOL_GROUP)
            span *= 2
        acc = src[data:data + tq, cols] + src[data - span:data - span + tq, cols]
        cnt = jnp.minimum(pos + 1, w).astype(F32)
        delta = (acc / cnt - u).astype(BF16)
        yp = jnp.dot(delta, wpool_ref[gi], preferred_element_type=F32)
        yp = yp * pscale_ref[:, c0:c0 + POOL_GROUP]
        mixbuf[:, c0:c0 + POOL_GROUP] = yp.astype(BF16)

    lane = lax.broadcasted_iota(I32, (2 * WINDOW, LANES), 1)
    low = lane < HEAD_DIM
    qi = lax.broadcasted_iota(I32, (WINDOW, 2 * WINDOW), 0)
    kj = lax.broadcasted_iota(I32, (WINDOW, 2 * WINDOW), 1)
    dist = qi - kj + WINDOW
    band = jnp.logical_and(dist >= 0, dist < WINDOW)
    olane = lax.broadcasted_iota(I32, (WINDOW, LANES), 1)
    k_col = D_POOL + D_ATTN
    v_col = k_col + D_KV
    for blk in range(tq // WINDOW):
        r0 = blk * WINDOW
        if blk == 0:
            kcat = jnp.concatenate(
                [kvp_ref[:, 0:D_KV], proj_ref[0:WINDOW, k_col:k_col + D_KV]], axis=0)
            vcat = jnp.concatenate(
                [kvp_ref[:, D_KV:2 * D_KV], proj_ref[0:WINDOW, v_col:v_col + D_KV]], axis=0)
            mask = jnp.logical_and(band, kj >= jnp.where(seq_start, WINDOW, 0))
        else:
            kcat = proj_ref[r0 - WINDOW:r0 + WINDOW, k_col:k_col + D_KV]
            vcat = proj_ref[r0 - WINDOW:r0 + WINDOW, v_col:v_col + D_KV]
            mask = band
        kswap = pltpu.roll(kcat.astype(F32), HEAD_DIM, 1).astype(BF16)
        vswap = pltpu.roll(vcat.astype(F32), HEAD_DIM, 1).astype(BF16)
        zero = jnp.zeros_like(kcat)
        for g in range(N_KV_HEADS):
            ksrc_lo, ksrc_hi = (kcat, kswap) if g == 0 else (kswap, kcat)
            vsrc_lo, vsrc_hi = (vcat, vswap) if g == 0 else (vswap, vcat)
            kbd = jnp.concatenate([jnp.where(low, ksrc_lo, zero),
                                   jnp.where(low, zero, ksrc_hi)], axis=0)
            vbd = jnp.concatenate([jnp.where(low, vsrc_lo, zero),
                                   jnp.where(low, zero, vsrc_hi)], axis=0)
            q_col = D_POOL + g * GQA_GROUP * HEAD_DIM
            q = jnp.concatenate(
                [proj_ref[r0:r0 + WINDOW, q_col + p * LANES:q_col + (p + 1) * LANES]
                 for p in range(PAIRS_PER_KV)], axis=0)
            q = q * jnp.asarray(HEAD_DIM ** -0.5, BF16)
            s = lax.dot_general(q, kbd, (((1,), (1,)), ((), ())),
                                preferred_element_type=F32)
            probs = []
            rdens = []
            for p in range(PAIRS_PER_KV):
                row_p = []
                row_r = []
                for hh in range(HEADS_PER_VREG):
                    sink = sinks_ref[g * GQA_GROUP + p * HEADS_PER_VREG + hh]
                    sp = s[p * WINDOW:(p + 1) * WINDOW,
                           hh * 2 * WINDOW:(hh + 1) * 2 * WINDOW]
                    sp = jnp.where(mask, sp, NEG_INF)
                    m = jnp.maximum(jnp.max(sp, axis=-1, keepdims=True), sink)
                    e = jnp.exp(sp - m)
                    den = jnp.sum(e, axis=-1, keepdims=True) + jnp.exp(sink - m)
                    row_p.append(e.astype(BF16))
                    row_r.append(1.0 / den)
                probs.append(jnp.concatenate(row_p, axis=1))
                rdens.append(row_r)
            pmat = jnp.concatenate(probs, axis=0)
            o = jnp.dot(pmat, vbd, preferred_element_type=F32)
            for p in range(PAIRS_PER_KV):
                op = o[p * WINDOW:(p + 1) * WINDOW, :]
                norm = jnp.where(olane < HEAD_DIM, rdens[p][0], rdens[p][1])
                c0 = D_POOL + g * GQA_GROUP * HEAD_DIM + p * LANES
                mixbuf[r0:r0 + WINDOW, c0:c0 + LANES] = (op * norm).astype(BF16)

    y = jnp.dot(mixbuf[...], wout_ref[...], preferred_element_type=F32)
    x1 = x_ref[...] + mod_ref[0, 2:3, :] * y
    x1_ref[...] = x1

    h2 = _norm_mod(x1, g2_ref[...], mod_ref[0, 3:4, :], mod_ref[0, 4:5, :])
    _store_slabs(h2_ref, h2)

    logits = jnp.dot(h2.astype(BF16), wr_ref[...], preferred_element_type=F32) + br_ref[...]
    ln = lax.broadcasted_iota(I32, (tq, ROUTER_LANES), 1)
    lnf = ln.astype(F32)
    ninf = -jnp.inf
    is_g = ln < N_EXPERT_GROUPS
    gl = jnp.where(is_g, logits, ninf)
    gmax = jnp.max(gl, axis=-1, keepdims=True)
    g_sel = jnp.min(jnp.where(gl == gmax, lnf, float(ROUTER_LANES)), axis=-1, keepdims=True)
    p_grp = 1.0 / jnp.sum(jnp.where(is_g, jnp.exp(logits - gmax), 0.0), axis=-1, keepdims=True)
    lane_grp = ((ln - N_EXPERT_GROUPS) >> 3).astype(F32)
    in_sel = jnp.logical_and(
        jnp.logical_and(ln >= N_EXPERT_GROUPS, ln < N_EXPERT_GROUPS + N_EXPERTS),
        lane_grp == g_sel)
    el = jnp.where(in_sel, logits, ninf)
    l1 = jnp.max(el, axis=-1, keepdims=True)
    i1 = jnp.min(jnp.where(el == l1, lnf, float(ROUTER_LANES)), axis=-1, keepdims=True)
    el2 = jnp.where(lnf == i1, ninf, el)
    l2 = jnp.max(el2, axis=-1, keepdims=True)
    i2 = jnp.min(jnp.where(el2 == l2, lnf, float(ROUTER_LANES)), axis=-1, keepdims=True)
    tt = jnp.exp(l2 - l1)
    w1 = 1.0 / (1.0 + tt)
    comb1 = p_grp * w1
    comb2 = p_grp * (tt * w1)
    hit1 = lnf == i1
    hit2 = lnf == i2
    onehot = jnp.where(jnp.logical_or(hit1, hit2), 1.0, 0.0)
    rr = lax.broadcasted_iota(I32, (tq, tq), 0)
    cc = lax.broadcasted_iota(I32, (tq, tq), 1)
    lower = jnp.where(rr > cc, 1.0, 0.0).astype(BF16)
    prior = jnp.dot(lower, onehot.astype(BF16), preferred_element_type=F32) + carry[...]
    rank1 = jnp.sum(jnp.where(hit1, prior, 0.0), axis=-1, keepdims=True)
    rank2 = jnp.sum(jnp.where(hit2, prior, 0.0), axis=-1, keepdims=True)
    carry[...] = carry[...] + jnp.sum(onehot, axis=0, keepdims=True)
    cnt_ref[...] = carry[...].astype(I32)
    eid1 = i1 - float(N_EXPERT_GROUPS)
    eid2 = i2 - float(N_EXPERT_GROUPS)
    meta = jnp.where(ln == 0, eid1, jnp.where(ln == 1, eid2,
                     jnp.where(ln == 2, rank1, jnp.where(ln == 3, rank2, 0.0))))
    mi_ref[...] = meta.T[0:META_ROWS, :].astype(I32)
    mf_ref[...] = jnp.where(ln == 0, comb1, jnp.where(ln == 1, comb2, 0.0))


def _mix(sinks, proj, x2, mod3, w_pool, pool_scale, w_out, g2, w_r, b_r, bsz, seq):
    t, d = x2.shape
    n_in = proj.shape[1]
    per_b = seq // MIX_TQ
    q_per_win = MIX_TQ // WINDOW
    q_per_halo = MIX_TQ // MAX_POOL_WINDOW
    slabs = d // (2 * LANES)

    def row(b, j):
        return b * per_b + j

    out_shapes = (
        jax.ShapeDtypeStruct((t, d), F32),
        jax.ShapeDtypeStruct((t * slabs, LANES), U32),
        jax.ShapeDtypeStruct((META_ROWS, t), I32),
        jax.ShapeDtypeStruct((t, ROUTER_LANES), F32),
        jax.ShapeDtypeStruct((1, ROUTER_LANES), I32),
    )
    return pl.pallas_call(
        _mix_kernel,
        out_shape=out_shapes,
        grid=(bsz, per_b),
        in_specs=[
            pl.BlockSpec(memory_space=pltpu.SMEM),
            pl.BlockSpec((MIX_TQ, n_in), lambda b, j: (row(b, j), 0)),
            pl.BlockSpec((WINDOW, 2 * D_KV),
                         lambda b, j: (jnp.maximum(row(b, j) * q_per_win - 1, 0),
                                       (D_POOL + D_ATTN) // (2 * D_KV))),
            pl.BlockSpec((MAX_POOL_WINDOW, D_POOL),
                         lambda b, j: (jnp.maximum(row(b, j) * q_per_halo - 1, 0), 0)),
            pl.BlockSpec((MIX_TQ, d), lambda b, j: (row(b, j), 0)),
            pl.BlockSpec((1, N_MOD, d), lambda b, j: (b, 0, 0)),
            pl.BlockSpec(w_pool.shape, lambda b, j: (0, 0, 0)),
            pl.BlockSpec((1, D_POOL), lambda b, j: (0, 0)),
            pl.BlockSpec((d, d), lambda b, j: (0, 0)),
            pl.BlockSpec((1, d), lambda b, j: (0, 0)),
            pl.BlockSpec((d, ROUTER_LANES), lambda b, j: (0, 0)),
            pl.BlockSpec((1, ROUTER_LANES), lambda b, j: (0, 0)),
        ],
        out_specs=(
            pl.BlockSpec((MIX_TQ, d), lambda b, j: (row(b, j), 0)),
            pl.BlockSpec((MIX_TQ * slabs, LANES), lambda b, j: (row(b, j), 0)),
            pl.BlockSpec((META_ROWS, MIX_TQ), lambda b, j: (0, row(b, j))),
            pl.BlockSpec((MIX_TQ, ROUTER_LANES), lambda b, j: (row(b, j), 0)),
            pl.BlockSpec((1, ROUTER_LANES), lambda b, j: (0, 0)),
        ),
        scratch_shapes=[
            pltpu.VMEM((SUBLANES + MAX_POOL_WINDOW + MIX_TQ, D_POOL), F32),
            pltpu.VMEM((SUBLANES + MAX_POOL_WINDOW + MIX_TQ, POOL_GROUP), F32),
            pltpu.VMEM((SUBLANES + MAX_POOL_WINDOW + MIX_TQ, POOL_GROUP), F32),
            pltpu.VMEM((MIX_TQ, d), BF16),
            pltpu.VMEM((1, ROUTER_LANES), F32),
        ],
        compiler_params=pltpu.CompilerParams(
            dimension_semantics=("arbitrary", "arbitrary"), vmem_limit_bytes=VMEM_LIMIT),
        name="mix",
    )(sinks, proj, proj, proj, x2, mod3, w_pool, pool_scale, w_out, g2, w_r, b_r)


def _dest_kernel(pstart_ref, mi_ref, o_ref):
    e1 = mi_ref[0:1, :]
    e2 = mi_ref[1:2, :]
    base1 = jnp.zeros_like(e1)
    base2 = jnp.zeros_like(e2)
    for e in range(N_EXPERTS):
        start = pstart_ref[e]
        base1 = jnp.where(e1 == e, start, base1)
        base2 = jnp.where(e2 == e, start, base2)
    fields = lax.broadcasted_iota(I32, o_ref.shape, 0)
    o_ref[...] = jnp.where(fields == 0, base1 + mi_ref[2:3, :],
                           jnp.where(fields == 1, base2 + mi_ref[3:4, :], 0))


def _dest(pstarts, mi):
    return pl.pallas_call(
        _dest_kernel,
        out_shape=jax.ShapeDtypeStruct(mi.shape, I32),
        in_specs=[pl.BlockSpec(memory_space=pltpu.SMEM),
                  pl.BlockSpec(mi.shape, lambda: (0, 0))],
        out_specs=pl.BlockSpec(mi.shape, lambda: (0, 0)),
        name="dest",
    )(pstarts, mi)


def _dispatch_kernel(pstart_ref, count_ref, d1_ref, d2_ref, h2_ref, xr_hbm, zbuf, sem, zsem):
    tc = d1_ref.shape[0]
    step = pl.program_id(0)
    nblk = xr_hbm.shape[0] // MOE_BLK
    last = N_EXPERTS - 1
    n_used = (pstart_ref[last] + count_ref[last] + MOE_BLK - 1) // MOE_BLK

    def pad_copies(e):
        cnt = count_ref[e]
        npad = (-cnt) & (MOE_BLK - 1)
        off = pstart_ref[e] + cnt
        out = []
        for piece in PAD_PIECES:
            out.append((npad & piece, pltpu.make_async_copy(
                zbuf.at[pl.ds(0, piece)], xr_hbm.at[pl.ds(off, piece)], zsem)))
            off = off + (npad & piece)
        return out

    def tail_copy(blk):
        return pltpu.make_async_copy(zbuf, xr_hbm.at[pl.ds(blk * MOE_BLK, MOE_BLK)], zsem)

    def zero_fill(start):
        def per_expert(e, carry):
            for flag, cp in pad_copies(e):
                pl.when(flag != 0)(cp.start if start else cp.wait)
            return carry

        def per_tail(blk, carry):
            cp = tail_copy(blk)
            cp.start() if start else cp.wait()
            return carry

        lax.fori_loop(0, N_EXPERTS, per_expert, 0)
        lax.fori_loop(n_used, nblk, per_tail, 0)

    @pl.when(step == 0)
    def _():
        zbuf[...] = jnp.zeros_like(zbuf)
        zero_fill(True)

    def copies(tl):
        src = h2_ref.at[tl]
        return (pltpu.make_async_copy(src, xr_hbm.at[d1_ref[tl]], sem),
                pltpu.make_async_copy(src, xr_hbm.at[d2_ref[tl]], sem))

    def issue(i, carry):
        for u in range(DISPATCH_UNROLL):
            for k, cp in enumerate(copies(i * DISPATCH_UNROLL + u)):
                cp.start(priority=(2 * u + k) % DMA_THREADS)
        return carry

    def drain(i, carry):
        for u in range(DMA_UNROLL):
            for cp in copies(i * DMA_UNROLL + u):
                cp.wait()
        return carry

    lax.fori_loop(0, tc // DISPATCH_UNROLL, issue, 0)
    lax.fori_loop(0, tc // DMA_UNROLL, drain, 0)

    @pl.when(step == 0)
    def _():
        zero_fill(False)


def _dispatch(pstarts, counts, d1, d2, h2r, n_rows):
    t = d1.shape[0]
    tc = min(DISPATCH_TC, t)
    smem_blk = pl.BlockSpec((tc,), lambda i: (i,), memory_space=pltpu.SMEM)
    return pl.pallas_call(
        _dispatch_kernel,
        out_shape=jax.ShapeDtypeStruct((n_rows,) + h2r.shape[1:], h2r.dtype),
        grid=(t // tc,),
        in_specs=[
            pl.BlockSpec(memory_space=pltpu.SMEM),
            pl.BlockSpec(memory_space=pltpu.SMEM),
            smem_blk, smem_blk,
            pl.BlockSpec((tc,) + h2r.shape[1:], lambda i: (i, 0, 0)),
        ],
        out_specs=pl.BlockSpec(memory_space=pl.ANY),
        scratch_shapes=[pltpu.VMEM((MOE_BLK,) + h2r.shape[1:], h2r.dtype),
                        pltpu.SemaphoreType.DMA(()), pltpu.SemaphoreType.DMA(())],
        compiler_params=pltpu.CompilerParams(
            dimension_semantics=("arbitrary",), has_side_effects=True,
            vmem_limit_bytes=VMEM_LIMIT),
        name="dispatch",
    )(pstarts, counts, d1, d2, h2r)


def _experts_kernel(blk_e_ref, blk_row_ref, blk_valid_ref, blk_first_ref, blk_slot_ref,
                    *rest):
    ahead_refs = rest[:WEIGHT_SLOTS - 1]
    (xr_ref, wg_hbm, wu_hbm, wd_hbm, yr_ref,
     wg_f32, wu_f32, wd_f32, wg_bf, wu_bf, wd_bf, xb, wsems) = rest[WEIGHT_SLOTS - 1:]
    rows = xb.shape[0]
    words = xr_ref.shape[0] // EXPERT_STEP_BLOCKS

    def weight_copies(expert, s):
        return [pltpu.make_async_copy(src.at[expert], dst.at[s], wsems.at[s])
                for src, dst in ((wg_hbm, wg_f32), (wu_hbm, wu_f32), (wd_hbm, wd_f32))]

    def one_block(blk, x_ref, y_ref, very_first):
        nvalid = blk_valid_ref[blk]
        slot = blk_slot_ref[blk]

        def request(expert, k):
            s = slot + k
            s = jnp.where(s >= WEIGHT_SLOTS, s - WEIGHT_SLOTS, s)

            @pl.when(expert >= 0)
            def _():
                for cp in weight_copies(expert, s):
                    cp.start()

        if very_first is not None:
            @pl.when(very_first)
            def _():
                request(blk_e_ref[0], 0)
                for k in range(1, WEIGHT_SLOTS - 1):
                    request(ahead_refs[k - 1][0], k)

        @pl.when(blk_first_ref[blk] == 1)
        def _():
            request(ahead_refs[WEIGHT_SLOTS - 2][blk], WEIGHT_SLOTS - 1)
            for cp in weight_copies(blk_e_ref[blk], slot):
                cp.wait()
            wg_bf[...] = wg_f32[slot].astype(BF16)
            wu_bf[...] = wu_f32[slot].astype(BF16)
            wd_bf[...] = wd_f32[slot].astype(BF16)

        @pl.when(nvalid > 0)
        def _():
            for sidx, slab in enumerate(_load_slabs(x_ref, rows)):
                xb[:, sidx * LANES:(sidx + 1) * LANES] = slab.astype(BF16)
            xv = xb[...]
            gate = jnp.dot(xv, wg_bf[...], preferred_element_type=F32)
            up = jnp.dot(xv, wu_bf[...], preferred_element_type=F32)
            act = (_silu(gate) * up).astype(BF16)
            _store_slabs(y_ref, jnp.dot(act, wd_bf[...], preferred_element_type=F32))

        @pl.when(nvalid == 0)
        def _():
            _store_slabs(y_ref, jnp.zeros((rows, xb.shape[1]), F32))

    step = pl.program_id(0)
    for h in range(EXPERT_STEP_BLOCKS):
        part = pl.ds(h * words, words)
        one_block(step * EXPERT_STEP_BLOCKS + h, xr_ref.at[part], yr_ref.at[part],
                  step == 0 if h == 0 else None)


def _experts(schedule, xr, w_gate, w_up, w_down):
    n_rows, slabs, _ = xr.shape
    d = 2 * slabs * LANES
    nblk = n_rows // MOE_BLK
    per_step = EXPERT_STEP_BLOCKS
    assert nblk % per_step == 0, nblk
    grid_spec = pltpu.PrefetchScalarGridSpec(
        num_scalar_prefetch=len(schedule),
        grid=(nblk // per_step,),
        in_specs=[
            pl.BlockSpec((per_step * MOE_BLK * slabs, LANES),
                         lambda i, *sched: (sched[1][i * per_step] // per_step, 0)),
            pl.BlockSpec(memory_space=pl.ANY),
            pl.BlockSpec(memory_space=pl.ANY),
            pl.BlockSpec(memory_space=pl.ANY),
        ],
        out_specs=pl.BlockSpec((per_step * MOE_BLK * slabs, LANES), lambda i, *sched: (i, 0)),
        scratch_shapes=[
            pltpu.VMEM((WEIGHT_SLOTS, d, D_EXPERT), F32),
            pltpu.VMEM((WEIGHT_SLOTS, d, D_EXPERT), F32),
            pltpu.VMEM((WEIGHT_SLOTS, D_EXPERT, d), F32),
            pltpu.VMEM((d, D_EXPERT), BF16),
            pltpu.VMEM((d, D_EXPERT), BF16),
            pltpu.VMEM((D_EXPERT, d), BF16),
            pltpu.VMEM((MOE_BLK, d), BF16),
            pltpu.SemaphoreType.DMA((WEIGHT_SLOTS,)),
        ],
    )
    yr = pl.pallas_call(
        _experts_kernel,
        out_shape=jax.ShapeDtypeStruct((n_rows * slabs, LANES), U32),
        grid_spec=grid_spec,
        compiler_params=pltpu.CompilerParams(
            dimension_semantics=("arbitrary",), vmem_limit_bytes=VMEM_LIMIT),
        name="experts",
    )(*schedule, xr.reshape(n_rows * slabs, LANES), w_gate, w_up, w_down)
    return yr.reshape(n_rows, slabs, LANES)


def _combine_kernel(d1_ref, d2_ref, yr_hbm, x1_ref, mf_ref, mod_ref, gf_ref, o_ref,
                    ya0, yb0, ya1, yb1, sems):
    i = pl.program_id(0)
    n = pl.num_programs(0)
    tk = x1_ref.shape[0]
    bufs = ((ya0, yb0), (ya1, yb1))

    slabs = yr_hbm.shape[1]

    def copies(tile, slot, tl):
        tok = tile * tk + tl
        ya, yb = bufs[slot]
        dst = pl.ds(pl.multiple_of(tl * slabs, slabs), slabs)
        return (pltpu.make_async_copy(yr_hbm.at[d1_ref[tok]], ya.at[dst], sems.at[slot]),
                pltpu.make_async_copy(yr_hbm.at[d2_ref[tok]], yb.at[dst], sems.at[slot]))

    def issue(tile, slot):
        def body(k, carry):
            for u in range(DMA_UNROLL):
                for c, cp in enumerate(copies(tile, slot, k * DMA_UNROLL + u)):
                    cp.start(priority=(2 * u + c) % DMA_THREADS)
            return carry
        lax.fori_loop(0, tk // DMA_UNROLL, body, 0)

    def drain(tile, slot):
        def body(k, carry):
            for u in range(DMA_UNROLL):
                for cp in copies(tile, slot, k * DMA_UNROLL + u):
                    cp.wait()
            return carry
        lax.fori_loop(0, tk // DMA_UNROLL, body, 0)

    def step(slot):
        @pl.when(i == 0)
        def _():
            issue(i, slot)

        drain(i, slot)
        nxt = jnp.minimum(i + 1, n - 1)
        for tl in range(tk):
            for c, cp in enumerate(copies(nxt, 1 - slot, tl)):
                cp.start(priority=(2 * tl + c) % DMA_THREADS)
        ya, yb = bufs[slot]
        c1 = mf_ref[:, 0:1]
        c2 = mf_ref[:, 1:2]
        moe = jnp.concatenate(
            [c1 * sa + c2 * sb for sa, sb in zip(_load_slabs(ya, tk), _load_slabs(yb, tk))], axis=1)
        xo = x1_ref[...] + mod_ref[0, 5:6, :] * moe
        ms = jnp.mean(xo * xo, axis=-1, keepdims=True)
        o_ref[...] = (xo * lax.rsqrt(ms + RMS_EPS)) * gf_ref[...]

        @pl.when(i == n - 1)
        def _():
            drain(nxt, 1 - slot)

    @pl.when(i % 2 == 0)
    def _():
        step(0)

    @pl.when(i % 2 == 1)
    def _():
        step(1)


def _combine(d1, d2, yr, x1, mf, mod3, g_final, seq):
    t, d = x1.shape
    slabs = yr.shape[1]
    per_b = seq // COMBINE_TK
    grid_spec = pltpu.PrefetchScalarGridSpec(
        num_scalar_prefetch=2,
        grid=(t // COMBINE_TK,),
        in_specs=[
            pl.BlockSpec(memory_space=pl.ANY),
            pl.BlockSpec((COMBINE_TK, d), lambda i, a, b: (i, 0)),
            pl.BlockSpec((COMBINE_TK, ROUTER_LANES), lambda i, a, b: (i, 0)),
            pl.BlockSpec((1, N_MOD, d), lambda i, a, b: (i // per_b, 0, 0)),
            pl.BlockSpec((1, d), lambda i, a, b: (0, 0)),
        ],
        out_specs=pl.BlockSpec((COMBINE_TK, d), lambda i, a, b: (i, 0)),
        scratch_shapes=[pltpu.VMEM((COMBINE_TK * slabs, LANES), U32) for _ in range(4)]
        + [pltpu.SemaphoreType.DMA((2,))],
    )
    return pl.pallas_call(
        _combine_kernel,
        out_shape=jax.ShapeDtypeStruct((t, d), F32),
        grid_spec=grid_spec,
        compiler_params=pltpu.CompilerParams(
            dimension_semantics=("arbitrary",), vmem_limit_bytes=VMEM_LIMIT),
        name="combine",
    )(d1, d2, yr, x1, mf, mod3, g_final)


def _block_schedule(counts, n_rows):
    nblk = n_rows // MOE_BLK
    eidx = jnp.arange(N_EXPERTS, dtype=I32)
    bidx = jnp.arange(nblk, dtype=I32)
    nblocks = (counts + MOE_BLK - 1) // MOE_BLK
    bends = jnp.cumsum(nblocks)
    bstarts = bends - nblocks
    n_used = bends[-1]
    used = bidx < n_used
    blk_row = jnp.minimum(bidx, jnp.maximum(n_used - 1, 0))
    owner = jnp.logical_and(blk_row[:, None] >= bstarts[None, :], blk_row[:, None] < bends[None, :])

    def pick(per_expert):
        return jnp.sum(jnp.where(owner, per_expert[None, :], 0), axis=1).astype(I32)

    blk_e = pick(eidx)
    blk_valid = jnp.where(
        used, jnp.clip(pick(counts) - (bidx - pick(bstarts)) * MOE_BLK, 0, MOE_BLK), 0).astype(I32)
    blk_first = jnp.logical_and(used, bidx == pick(bstarts)).astype(I32)
    nonempty = counts > 0
    ordinal = jnp.cumsum(nonempty.astype(I32)) - 1
    blk_slot = pick(ordinal) % WEIGHT_SLOTS

    def ahead(k):
        match = jnp.logical_and(nonempty[None, :], ordinal[None, :] == ordinal[:, None] + k)
        found = jnp.sum(jnp.where(match, eidx[None, :], 0), axis=1)
        return pick(jnp.where(ordinal + k <= ordinal[-1], found, -1))

    blk_ahead = tuple(ahead(k) for k in range(1, WEIGHT_SLOTS))
    pstarts = (bstarts * MOE_BLK).astype(I32)
    return pstarts, (blk_e, blk_row, blk_valid, blk_first, blk_slot) + blk_ahead


def kernel(x, c, w_ada, b_ada, g_norm1, w_in, w_pool, pool_scale, attn_sinks, w_out,
           g_norm2, w_router_group, b_router_group, w_router_expert, b_router_expert,
           w_gate, w_up, w_down, g_final):
    bsz, seq, d = x.shape
    t = bsz * seq
    assert w_ada.shape[0] == 1, "single-layer model: the combine step applies the final norm"
    assert seq % INPROJ_TM == 0 and seq % MIX_TQ == 0 and seq % COMBINE_TK == 0, seq
    assert t % min(DISPATCH_TC, t) == 0 and d % (2 * LANES) == 0, (t, d)
    x2 = x.reshape(t, d)
    for l in range(1):
        mod3 = _ada(c, w_ada[l], b_ada[l]).reshape(bsz, N_MOD, d)
        proj = _inproj(x2, mod3, g_norm1[l].reshape(1, d), w_in[l].astype(BF16), seq)
        pad = ROUTER_LANES - N_EXPERT_GROUPS - N_EXPERTS
        w_r = jnp.concatenate(
            [w_router_group[l], w_router_expert[l], jnp.zeros((d, pad), F32)], axis=1).astype(BF16)
        b_r = jnp.concatenate(
            [b_router_group[l], b_router_expert[l], jnp.zeros((pad,), F32)]).reshape(1, ROUTER_LANES)
        x1, h2r, mi, mf, cnt = _mix(
            attn_sinks[l], proj, x2, mod3, w_pool[l].astype(BF16),
            pool_scale[l].reshape(1, D_POOL), w_out[l].astype(BF16),
            g_norm2[l].reshape(1, d), w_r, b_r, bsz, seq)
        n_rows = 2 * t + N_EXPERTS * MOE_BLK
        counts = cnt[0, N_EXPERT_GROUPS:N_EXPERT_GROUPS + N_EXPERTS]
        pstarts, schedule = _block_schedule(counts, n_rows)
        dest = _dest(pstarts, mi)
        d1, d2 = dest[0], dest[1]
        xr = _dispatch(pstarts, counts, d1, d2, h2r.reshape(t, d // (2 * LANES), LANES), n_rows)
        yr = _experts(schedule, xr, w_gate[l], w_up[l], w_down[l])
        x2 = _combine(d1, d2, yr, x1, mf, mod3, g_final.reshape(1, d), seq)
    return x2.reshape(bsz, seq, d)
```

```python
import functools

import jax
import jax.numpy as jnp
from jax import lax
from jax.experimental import pallas as pl
from jax.experimental.pallas import tpu as pltpu

F32 = jnp.float32
BF16 = jnp.bfloat16
I32 = jnp.int32
U32 = jnp.uint32

D_POOL = 1024
POOL_WINDOWS = (2, 4, 8, 16)
POOL_GROUP = 256
MAX_POOL_WINDOW = 16
HEAD_DIM = 64
N_KV_HEADS = 2
GQA_GROUP = 8
WINDOW = 128
D_ATTN = 1024
D_KV = 128
N_EXPERT_GROUPS = 8
EXPERTS_PER_GROUP = 8
N_EXPERTS = 64
D_EXPERT = 512
N_MOD = 6
RMS_EPS = 1e-6
NEG_INF = -1e30

LANES = 128
SUBLANES = 8
HEADS_PER_VREG = LANES // HEAD_DIM
PAIRS_PER_KV = GQA_GROUP // HEADS_PER_VREG

ADA_TN = 1024
INPROJ_TM = 1024
MIX_TQ = 512
MOE_BLK = 256
WEIGHT_SLOTS = 2
EXPERT_STEP_BLOCKS = 2
PAD_PIECES = tuple(MOE_BLK >> (k + 1) for k in range(MOE_BLK.bit_length() - 1))
DISPATCH_TC = 2048
COMBINE_TK = 512
DMA_UNROLL = 8
DMA_THREADS = 2
DISPATCH_UNROLL = 64
ROUTER_LANES = 128
META_ROWS = 8
VMEM_LIMIT = 56 * 1024 * 1024


def _silu(v):
    return v * jax.nn.sigmoid(v)


def _load_slabs(ref, rows):
    slabs = ref.shape[0] // rows
    words = [ref[pl.ds(s, rows, stride=slabs), :] for s in range(slabs)]
    return [pltpu.unpack_elementwise(w, index=half, packed_dtype=BF16, unpacked_dtype=F32)
            for half in range(2) for w in words]


def _store_slabs(ref, val):
    rows, d = val.shape
    slabs = ref.shape[0] // rows
    for s in range(slabs):
        lo = val[:, s * LANES:(s + 1) * LANES]
        hi = val[:, d // 2 + s * LANES:d // 2 + (s + 1) * LANES]
        ref[pl.ds(s, rows, stride=slabs), :] = pltpu.pack_elementwise([lo, hi], packed_dtype=BF16)


def _ada_kernel(c_ref, w_ref, b_ref, o_ref):
    ca = _silu(c_ref[...])
    o_ref[...] = jnp.dot(ca.astype(BF16), w_ref[...].astype(BF16),
                         preferred_element_type=F32) + b_ref[...]


def _ada(c, w, b):
    bsz, d = c.shape
    n = w.shape[1]
    return pl.pallas_call(
        _ada_kernel,
        out_shape=jax.ShapeDtypeStruct((bsz, n), F32),
        grid=(n // ADA_TN,),
        in_specs=[
            pl.BlockSpec((bsz, d), lambda i: (0, 0)),
            pl.BlockSpec((d, ADA_TN), lambda i: (0, i)),
            pl.BlockSpec((1, ADA_TN), lambda i: (0, i)),
        ],
        out_specs=pl.BlockSpec((bsz, ADA_TN), lambda i: (0, i)),
        compiler_params=pltpu.CompilerParams(
            dimension_semantics=("arbitrary",), vmem_limit_bytes=VMEM_LIMIT),
        name="ada",
    )(c, w, b.reshape(1, n))


def _norm_mod(x, g, shift, scale):
    ms = jnp.mean(x * x, axis=-1, keepdims=True)
    return (x * lax.rsqrt(ms + RMS_EPS)) * (g * (1.0 + scale)) + shift


def _inproj_kernel(x_ref, mod_ref, g_ref, w_ref, o_ref):
    h = _norm_mod(x_ref[...], g_ref[...], mod_ref[0, 0:1, :], mod_ref[0, 1:2, :])
    o_ref[...] = jnp.dot(h.astype(BF16), w_ref[...],
                         preferred_element_type=F32).astype(o_ref.dtype)


def _inproj(x2, mod3, g1, w_in, seq):
    t, d = x2.shape
    n = w_in.shape[1]
    per_b = seq // INPROJ_TM
    return pl.pallas_call(
        _inproj_kernel,
        out_shape=jax.ShapeDtypeStruct((t, n), BF16),
        grid=(t // INPROJ_TM,),
        in_specs=[
            pl.BlockSpec((INPROJ_TM, d), lambda i: (i, 0)),
            pl.BlockSpec((1, N_MOD, d), lambda i: (i // per_b, 0, 0)),
            pl.BlockSpec((1, d), lambda i: (0, 0)),
            pl.BlockSpec((d, n), lambda i: (0, 0)),
        ],
        out_specs=pl.BlockSpec((INPROJ_TM, n), lambda i: (i, 0)),
        compiler_params=pltpu.CompilerParams(
            dimension_semantics=("arbitrary",), vmem_limit_bytes=VMEM_LIMIT),
        name="inproj",
    )(x2, mod3, g1, w_in)


def _mix_kernel(sinks_ref, proj_ref, kvp_ref, up_ref, x_ref, mod_ref, wpool_ref,
                pscale_ref, wout_ref, g2_ref, wr_ref, br_ref,
                x1_ref, h2_ref, mi_ref, mf_ref, cnt_ref,
                ubuf, lvl_a, lvl_b, mixbuf, carry):
    b = pl.program_id(0)
    j = pl.program_id(1)
    tq = x_ref.shape[0]
    seq_start = j == 0

    @pl.when(jnp.logical_and(b == 0, j == 0))
    def _():
        carry[...] = jnp.zeros_like(carry)

    top = SUBLANES
    data = top + MAX_POOL_WINDOW
    ext = MAX_POOL_WINDOW + tq
    halo = up_ref[...].astype(F32)
    ubuf[0:top, :] = jnp.zeros((top, D_POOL), F32)
    ubuf[top:data, :] = jnp.where(seq_start, 0.0, halo)
    ubuf[data:data + tq, :] = proj_ref[:, 0:D_POOL].astype(F32)
    lvl_a[0:top, :] = jnp.zeros((top, POOL_GROUP), F32)
    lvl_b[0:top, :] = jnp.zeros((top, POOL_GROUP), F32)
    pos = j * tq + lax.broadcasted_iota(I32, (tq, 1), 0)
    for gi, w in enumerate(POOL_WINDOWS):
        c0 = gi * POOL_GROUP
        u = ubuf[data:data + tq, c0:c0 + POOL_GROUP]
        src, cols = ubuf, slice(c0, c0 + POOL_GROUP)
        span = 1
        while 2 * span < w:
            dst = lvl_b if src is lvl_a else lvl_a
            dst[top:top + ext, :] = src[top:top + ext, cols] + src[top - span:top - span + ext, cols]
            src, cols = dst, slice(0, POOL_GROUP)
            span *= 2
        acc = src[data:data + tq, cols] + src[data - span:data - span + tq, cols]
        cnt = jnp.minimum(pos + 1, w).astype(F32)
        delta = (acc / cnt - u).astype(BF16)
        yp = jnp.dot(delta, wpool_ref[gi], preferred_element_type=F32)
        yp = yp * pscale_ref[:, c0:c0 + POOL_GROUP]
        mixbuf[:, c0:c0 + POOL_GROUP] = yp.astype(BF16)

    lane = lax.broadcasted_iota(I32, (2 * WINDOW, LANES), 1)
    low = lane < HEAD_DIM
    qi = lax.broadcasted_iota(I32, (WINDOW, 2 * WINDOW), 0)
    kj = lax.broadcasted_iota(I32, (WINDOW, 2 * WINDOW), 1)
    dist = qi - kj + WINDOW
    band = jnp.logical_and(dist >= 0, dist < WINDOW)
    olane = lax.broadcasted_iota(I32, (WINDOW, LANES), 1)
    k_col = D_POOL + D_ATTN
    v_col = k_col + D_KV
    for blk in range(tq // WINDOW):
        r0 = blk * WINDOW
        if blk == 0:
            kcat = jnp.concatenate(
                [kvp_ref[:, 0:D_KV], proj_ref[0:WINDOW, k_col:k_col + D_KV]], axis=0)
            vcat = jnp.concatenate(
                [kvp_ref[:, D_KV:2 * D_KV], proj_ref[0:WINDOW, v_col:v_col + D_KV]], axis=0)
            mask = jnp.logical_and(band, kj >= jnp.where(seq_start, WINDOW, 0))
        else:
            kcat = proj_ref[r0 - WINDOW:r0 + WINDOW, k_col:k_col + D_KV]
            vcat = proj_ref[r0 - WINDOW:r0 + WINDOW, v_col:v_col + D_KV]
            mask = band
        kswap = pltpu.roll(kcat.astype(F32), HEAD_DIM, 1).astype(BF16)
        vswap = pltpu.roll(vcat.astype(F32), HEAD_DIM, 1).astype(BF16)
        zero = jnp.zeros_like(kcat)
        for g in range(N_KV_HEADS):
            ksrc_lo, ksrc_hi = (kcat, kswap) if g == 0 else (kswap, kcat)
            vsrc_lo, vsrc_hi = (vcat, vswap) if g == 0 else (vswap, vcat)
            kbd = jnp.concatenate([jnp.where(low, ksrc_lo, zero),
                                   jnp.where(low, zero, ksrc_hi)], axis=0)
            vbd = jnp.concatenate([jnp.where(low, vsrc_lo, zero),
                                   jnp.where(low, zero, vsrc_hi)], axis=0)
            q_col = D_POOL + g * GQA_GROUP * HEAD_DIM
            q = jnp.concatenate(
                [proj_ref[r0:r0 + WINDOW, q_col + p * LANES:q_col + (p + 1) * LANES]
                 for p in range(PAIRS_PER_KV)], axis=0)
            q = q * jnp.asarray(HEAD_DIM ** -0.5, BF16)
            s = lax.dot_general(q, kbd, (((1,), (1,)), ((), ())),
                                preferred_element_type=F32)
            probs = []
            rdens = []
            for p in range(PAIRS_PER_KV):
                row_p = []
                row_r = []
                for hh in range(HEADS_PER_VREG):
                    sink = sinks_ref[g * GQA_GROUP + p * HEADS_PER_VREG + hh]
                    sp = s[p * WINDOW:(p + 1) * WINDOW,
                           hh * 2 * WINDOW:(hh + 1) * 2 * WINDOW]
                    sp = jnp.where(mask, sp, NEG_INF)
                    m = jnp.maximum(jnp.max(sp, axis=-1, keepdims=True), sink)
                    e = jnp.exp(sp - m)
                    den = jnp.sum(e, axis=-1, keepdims=True) + jnp.exp(sink - m)
                    row_p.append(e.astype(BF16))
                    row_r.append(1.0 / den)
                probs.append(jnp.concatenate(row_p, axis=1))
                rdens.append(row_r)
            pmat = jnp.concatenate(probs, axis=0)
            o = jnp.dot(pmat, vbd, preferred_element_type=F32)
            for p in range(PAIRS_PER_KV):
                op = o[p * WINDOW:(p + 1) * WINDOW, :]
                norm = jnp.where(olane < HEAD_DIM, rdens[p][0], rdens[p][1])
                c0 = D_POOL + g * GQA_GROUP * HEAD_DIM + p * LANES
                mixbuf[r0:r0 + WINDOW, c0:c0 + LANES] = (op * norm).astype(BF16)

    y = jnp.dot(mixbuf[...], wout_ref[...], preferred_element_type=F32)
    x1 = x_ref[...] + mod_ref[0, 2:3, :] * y
    x1_ref[...] = x1

    h2 = _norm_mod(x1, g2_ref[...], mod_ref[0, 3:4, :], mod_ref[0, 4:5, :])
    _store_slabs(h2_ref, h2)

    logits = jnp.dot(h2.astype(BF16), wr_ref[...], preferred_element_type=F32) + br_ref[...]
    ln = lax.broadcasted_iota(I32, (tq, ROUTER_LANES), 1)
    lnf = ln.astype(F32)
    ninf = -jnp.inf
    is_g = ln < N_EXPERT_GROUPS
    gl = jnp.where(is_g, logits, ninf)
    gmax = jnp.max(gl, axis=-1, keepdims=True)
    g_sel = jnp.min(jnp.where(gl == gmax, lnf, float(ROUTER_LANES)), axis=-1, keepdims=True)
    p_grp = 1.0 / jnp.sum(jnp.where(is_g, jnp.exp(logits - gmax), 0.0), axis=-1, keepdims=True)
    lane_grp = ((ln - N_EXPERT_GROUPS) >> 3).astype(F32)
    in_sel = jnp.logical_and(
        jnp.logical_and(ln >= N_EXPERT_GROUPS, ln < N_EXPERT_GROUPS + N_EXPERTS),
        lane_grp == g_sel)
    el = jnp.where(in_sel, logits, ninf)
    l1 = jnp.max(el, axis=-1, keepdims=True)
    i1 = jnp.min(jnp.where(el == l1, lnf, float(ROUTER_LANES)), axis=-1, keepdims=True)
    el2 = jnp.where(lnf == i1, ninf, el)
    l2 = jnp.max(el2, axis=-1, keepdims=True)
    i2 = jnp.min(jnp.where(el2 == l2, lnf, float(ROUTER_LANES)), axis=-1, keepdims=True)
    tt = jnp.exp(l2 - l1)
    w1 = 1.0 / (1.0 + tt)
    comb1 = p_grp * w1
    comb2 = p_grp * (tt * w1)
    hit1 = lnf == i1
    hit2 = lnf == i2
    onehot = jnp.where(jnp.logical_or(hit1, hit2), 1.0, 0.0)
    rr = lax.broadcasted_iota(I32, (tq, tq), 0)
    cc = lax.broadcasted_iota(I32, (tq, tq), 1)
    lower = jnp.where(rr > cc, 1.0, 0.0).astype(BF16)
    prior = jnp.dot(lower, onehot.astype(BF16), preferred_element_type=F32) + carry[...]
    rank1 = jnp.sum(jnp.where(hit1, prior, 0.0), axis=-1, keepdims=True)
    rank2 = jnp.sum(jnp.where(hit2, prior, 0.0), axis=-1, keepdims=True)
    carry[...] = carry[...] + jnp.sum(onehot, axis=0, keepdims=True)
    cnt_ref[...] = carry[...].astype(I32)
    eid1 = i1 - float(N_EXPERT_GROUPS)
    eid2 = i2 - float(N_EXPERT_GROUPS)
    meta = jnp.where(ln == 0, eid1, jnp.where(ln == 1, eid2,
                     jnp.where(ln == 2, rank1, jnp.where(ln == 3, rank2, 0.0))))
    mi_ref[...] = meta.T[0:META_ROWS, :].astype(I32)
    mf_ref[...] = jnp.where(ln == 0, comb1, jnp.where(ln == 1, comb2, 0.0))


def _mix(sinks, proj, x2, mod3, w_pool, pool_scale, w_out, g2, w_r, b_r, bsz, seq):
    t, d = x2.shape
    n_in = proj.shape[1]
    per_b = seq // MIX_TQ
    q_per_win = MIX_TQ // WINDOW
    q_per_halo = MIX_TQ // MAX_POOL_WINDOW
    slabs = d // (2 * LANES)

    def row(b, j):
        return b * per_b + j

    out_shapes = (
        jax.ShapeDtypeStruct((t, d), F32),
        jax.ShapeDtypeStruct((t * slabs, LANES), U32),
        jax.ShapeDtypeStruct((META_ROWS, t), I32),
        jax.ShapeDtypeStruct((t, ROUTER_LANES), F32),
        jax.ShapeDtypeStruct((1, ROUTER_LANES), I32),
    )
    return pl.pallas_call(
        _mix_kernel,
        out_shape=out_shapes,
        grid=(bsz, per_b),
        in_specs=[
            pl.BlockSpec(memory_space=pltpu.SMEM),
            pl.BlockSpec((MIX_TQ, n_in), lambda b, j: (row(b, j), 0)),
            pl.BlockSpec((WINDOW, 2 * D_KV),
                         lambda b, j: (jnp.maximum(row(b, j) * q_per_win - 1, 0),
                                       (D_POOL + D_ATTN) // (2 * D_KV))),
            pl.BlockSpec((MAX_POOL_WINDOW, D_POOL),
                         lambda b, j: (jnp.maximum(row(b, j) * q_per_halo - 1, 0), 0)),
            pl.BlockSpec((MIX_TQ, d), lambda b, j: (row(b, j), 0)),
            pl.BlockSpec((1, N_MOD, d), lambda b, j: (b, 0, 0)),
            pl.BlockSpec(w_pool.shape, lambda b, j: (0, 0, 0)),
            pl.BlockSpec((1, D_POOL), lambda b, j: (0, 0)),
            pl.BlockSpec((d, d), lambda b, j: (0, 0)),
            pl.BlockSpec((1, d), lambda b, j: (0, 0)),
            pl.BlockSpec((d, ROUTER_LANES), lambda b, j: (0, 0)),
            pl.BlockSpec((1, ROUTER_LANES), lambda b, j: (0, 0)),
        ],
        out_specs=(
            pl.BlockSpec((MIX_TQ, d), lambda b, j: (row(b, j), 0)),
            pl.BlockSpec((MIX_TQ * slabs, LANES), lambda b, j: (row(b, j), 0)),
            pl.BlockSpec((META_ROWS, MIX_TQ), lambda b, j: (0, row(b, j))),
            pl.BlockSpec((MIX_TQ, ROUTER_LANES), lambda b, j: (row(b, j), 0)),
            pl.BlockSpec((1, ROUTER_LANES), lambda b, j: (0, 0)),
        ),
        scratch_shapes=[
            pltpu.VMEM((SUBLANES + MAX_POOL_WINDOW + MIX_TQ, D_POOL), F32),
            pltpu.VMEM((SUBLANES + MAX_POOL_WINDOW + MIX_TQ, POOL_GROUP), F32),
            pltpu.VMEM((SUBLANES + MAX_POOL_WINDOW + MIX_TQ, POOL_GROUP), F32),
            pltpu.VMEM((MIX_TQ, d), BF16),
            pltpu.VMEM((1, ROUTER_LANES), F32),
        ],
        compiler_params=pltpu.CompilerParams(
            dimension_semantics=("arbitrary", "arbitrary"), vmem_limit_bytes=VMEM_LIMIT),
        name="mix",
    )(sinks, proj, proj, proj, x2, mod3, w_pool, pool_scale, w_out, g2, w_r, b_r)


def _dest_kernel(pstart_ref, mi_ref, o_ref):
    e1 = mi_ref[0:1, :]
    e2 = mi_ref[1:2, :]
    base1 = jnp.zeros_like(e1)
    base2 = jnp.zeros_like(e2)
    for e in range(N_EXPERTS):
        start = pstart_ref[e]
        base1 = jnp.where(e1 == e, start, base1)
        base2 = jnp.where(e2 == e, start, base2)
    fields = lax.broadcasted_iota(I32, o_ref.shape, 0)
    o_ref[...] = jnp.where(fields == 0, base1 + mi_ref[2:3, :],
                           jnp.where(fields == 1, base2 + mi_ref[3:4, :], 0))


def _dest(pstarts, mi):
    return pl.pallas_call(
        _dest_kernel,
        out_shape=jax.ShapeDtypeStruct(mi.shape, I32),
        in_specs=[pl.BlockSpec(memory_space=pltpu.SMEM),
                  pl.BlockSpec(mi.shape, lambda: (0, 0))],
        out_specs=pl.BlockSpec(mi.shape, lambda: (0, 0)),
        name="dest",
    )(pstarts, mi)


def _dispatch_kernel(pstart_ref, count_ref, d1_ref, d2_ref, h2_ref, xr_hbm, zbuf, sem, zsem):
    tc = d1_ref.shape[0]
    step = pl.program_id(0)
    nblk = xr_hbm.shape[0] // MOE_BLK
    last = N_EXPERTS - 1
    n_used = (pstart_ref[last] + count_ref[last] + MOE_BLK - 1) // MOE_BLK

    def pad_copies(e):
        cnt = count_ref[e]
        npad = (-cnt) & (MOE_BLK - 1)
        off = pstart_ref[e] + cnt
        out = []
        for piece in PAD_PIECES:
            out.append((npad & piece, pltpu.make_async_copy(
                zbuf.at[pl.ds(0, piece)], xr_hbm.at[pl.ds(off, piece)], zsem)))
            off = off + (npad & piece)
        return out

    def tail_copy(blk):
        return pltpu.make_async_copy(zbuf, xr_hbm.at[pl.ds(blk * MOE_BLK, MOE_BLK)], zsem)

    def zero_fill(start):
        def per_expert(e, carry):
            for flag, cp in pad_copies(e):
                pl.when(flag != 0)(cp.start if start else cp.wait)
            return carry

        def per_tail(blk, carry):
            cp = tail_copy(blk)
            cp.start() if start else cp.wait()
            return carry

        lax.fori_loop(0, N_EXPERTS, per_expert, 0)
        lax.fori_loop(n_used, nblk, per_tail, 0)

    @pl.when(step == 0)
    def _():
        zbuf[...] = jnp.zeros_like(zbuf)
        zero_fill(True)

    def copies(tl):
        src = h2_ref.at[tl]
        return (pltpu.make_async_copy(src, xr_hbm.at[d1_ref[tl]], sem),
                pltpu.make_async_copy(src, xr_hbm.at[d2_ref[tl]], sem))

    def issue(i, carry):
        for u in range(DISPATCH_UNROLL):
            for k, cp in enumerate(copies(i * DISPATCH_UNROLL + u)):
                cp.start(priority=(2 * u + k) % DMA_THREADS)
        return carry

    def drain(i, carry):
        for u in range(DMA_UNROLL):
            for cp in copies(i * DMA_UNROLL + u):
                cp.wait()
        return carry

    lax.fori_loop(0, tc // DISPATCH_UNROLL, issue, 0)
    lax.fori_loop(0, tc // DMA_UNROLL, drain, 0)

    @pl.when(step == 0)
    def _():
        zero_fill(False)


def _dispatch(pstarts, counts, d1, d2, h2r, n_rows):
    t = d1.shape[0]
    tc = min(DISPATCH_TC, t)
    smem_blk = pl.BlockSpec((tc,), lambda i: (i,), memory_space=pltpu.SMEM)
    return pl.pallas_call(
        _dispatch_kernel,
        out_shape=jax.ShapeDtypeStruct((n_rows,) + h2r.shape[1:], h2r.dtype),
        grid=(t // tc,),
        in_specs=[
            pl.BlockSpec(memory_space=pltpu.SMEM),
            pl.BlockSpec(memory_space=pltpu.SMEM),
            smem_blk, smem_blk,
            pl.BlockSpec((tc,) + h2r.shape[1:], lambda i: (i, 0, 0)),
        ],
        out_specs=pl.BlockSpec(memory_space=pl.ANY),
        scratch_shapes=[pltpu.VMEM((MOE_BLK,) + h2r.shape[1:], h2r.dtype),
                        pltpu.SemaphoreType.DMA(()), pltpu.SemaphoreType.DMA(())],
        compiler_params=pltpu.CompilerParams(
            dimension_semantics=("arbitrary",), has_side_effects=True,
            vmem_limit_bytes=VMEM_LIMIT),
        name="dispatch",
    )(pstarts, counts, d1, d2, h2r)


def _experts_kernel(blk_e_ref, blk_row_ref, blk_valid_ref, blk_first_ref, blk_slot_ref,
                    *rest):
    ahead_refs = rest[:WEIGHT_SLOTS - 1]
    (xr_ref, wg_hbm, wu_hbm, wd_hbm, yr_ref,
     wg_f32, wu_f32, wd_f32, wg_bf, wu_bf, wd_bf, xb, wsems) = rest[WEIGHT_SLOTS - 1:]
    rows = xb.shape[0]
    words = xr_ref.shape[0] // EXPERT_STEP_BLOCKS

    def weight_copies(expert, s):
        return [pltpu.make_async_copy(src.at[expert], dst.at[s], wsems.at[s])
                for src, dst in ((wg_hbm, wg_f32), (wu_hbm, wu_f32), (wd_hbm, wd_f32))]

    def one_block(blk, x_ref, y_ref, very_first):
        nvalid = blk_valid_ref[blk]
        slot = blk_slot_ref[blk]

        def request(expert, k):
            s = slot + k
            s = jnp.where(s >= WEIGHT_SLOTS, s - WEIGHT_SLOTS, s)

            @pl.when(expert >= 0)
            def _():
                for cp in weight_copies(expert, s):
                    cp.start()

        if very_first is not None:
            @pl.when(very_first)
            def _():
                request(blk_e_ref[0], 0)
                for k in range(1, WEIGHT_SLOTS - 1):
                    request(ahead_refs[k - 1][0], k)

        @pl.when(blk_first_ref[blk] == 1)
        def _():
            request(ahead_refs[WEIGHT_SLOTS - 2][blk], WEIGHT_SLOTS - 1)
            for cp in weight_copies(blk_e_ref[blk], slot):
                cp.wait()
            wg_bf[...] = wg_f32[slot].astype(BF16)
            wu_bf[...] = wu_f32[slot].astype(BF16)
            wd_bf[...] = wd_f32[slot].astype(BF16)

        @pl.when(nvalid > 0)
        def _():
            for sidx, slab in enumerate(_load_slabs(x_ref, rows)):
                xb[:, sidx * LANES:(sidx + 1) * LANES] = slab.astype(BF16)
            xv = xb[...]
            gate = jnp.dot(xv, wg_bf[...], preferred_element_type=F32)
            up = jnp.dot(xv, wu_bf[...], preferred_element_type=F32)
            act = (_silu(gate) * up).astype(BF16)
            _store_slabs(y_ref, jnp.dot(act, wd_bf[...], preferred_element_type=F32))

        @pl.when(nvalid == 0)
        def _():
            _store_slabs(y_ref, jnp.zeros((rows, xb.shape[1]), F32))

    step = pl.program_id(0)
    for h in range(EXPERT_STEP_BLOCKS):
        part = pl.ds(h * words, words)
        one_block(step * EXPERT_STEP_BLOCKS + h, xr_ref.at[part], yr_ref.at[part],
                  step == 0 if h == 0 else None)


def _experts(schedule, xr, w_gate, w_up, w_down):
    n_rows, slabs, _ = xr.shape
    d = 2 * slabs * LANES
    nblk = n_rows // MOE_BLK
    per_step = EXPERT_STEP_BLOCKS
    assert nblk % per_step == 0, nblk
    grid_spec = pltpu.PrefetchScalarGridSpec(
        num_scalar_prefetch=len(schedule),
        grid=(nblk // per_step,),
        in_specs=[
            pl.BlockSpec((per_step * MOE_BLK * slabs, LANES),
                         lambda i, *sched: (sched[1][i * per_step] // per_step, 0)),
            pl.BlockSpec(memory_space=pl.ANY),
            pl.BlockSpec(memory_space=pl.ANY),
            pl.BlockSpec(memory_space=pl.ANY),
        ],
        out_specs=pl.BlockSpec((per_step * MOE_BLK * slabs, LANES), lambda i, *sched: (i, 0)),
        scratch_shapes=[
            pltpu.VMEM((WEIGHT_SLOTS, d, D_EXPERT), F32),
            pltpu.VMEM((WEIGHT_SLOTS, d, D_EXPERT), F32),
            pltpu.VMEM((WEIGHT_SLOTS, D_EXPERT, d), F32),
            pltpu.VMEM((d, D_EXPERT), BF16),
            pltpu.VMEM((d, D_EXPERT), BF16),
            pltpu.VMEM((D_EXPERT, d), BF16),
            pltpu.VMEM((MOE_BLK, d), BF16),
            pltpu.SemaphoreType.DMA((WEIGHT_SLOTS,)),
        ],
    )
    yr = pl.pallas_call(
        _experts_kernel,
        out_shape=jax.ShapeDtypeStruct((n_rows * slabs, LANES), U32),
        grid_spec=grid_spec,
        compiler_params=pltpu.CompilerParams(
            dimension_semantics=("arbitrary",), vmem_limit_bytes=VMEM_LIMIT),
        name="experts",
    )(*schedule, xr.reshape(n_rows * slabs, LANES), w_gate, w_up, w_down)
    return yr.reshape(n_rows, slabs, LANES)


def _combine_kernel(d1_ref, d2_ref, yr_hbm, x1_ref, mf_ref, mod_ref, gf_ref, o_ref,
                    ya0, yb0, ya1, yb1, sems):
    i = pl.program_id(0)
    n = pl.num_programs(0)
    tk = x1_ref.shape[0]
    bufs = ((ya0, yb0), (ya1, yb1))

    slabs = yr_hbm.shape[1]

    def copies(tile, slot, tl):
        tok = tile * tk + tl
        ya, yb = bufs[slot]
        dst = pl.ds(pl.multiple_of(tl * slabs, slabs), slabs)
        return (pltpu.make_async_copy(yr_hbm.at[d1_ref[tok]], ya.at[dst], sems.at[slot]),
                pltpu.make_async_copy(yr_hbm.at[d2_ref[tok]], yb.at[dst], sems.at[slot]))

    def issue(tile, slot):
        def body(k, carry):
            for u in range(DMA_UNROLL):
                for c, cp in enumerate(copies(tile, slot, k * DMA_UNROLL + u)):
                    cp.start(priority=(2 * u + c) % DMA_THREADS)
            return carry
        lax.fori_loop(0, tk // DMA_UNROLL, body, 0)

    def drain(tile, slot):
        def body(k, carry):
            for u in range(DMA_UNROLL):
                for cp in copies(tile, slot, k * DMA_UNROLL + u):
                    cp.wait()
            return carry
        lax.fori_loop(0, tk // DMA_UNROLL, body, 0)

    def step(slot):
        @pl.when(i == 0)
        def _():
            issue(i, slot)

        drain(i, slot)
        nxt = jnp.minimum(i + 1, n - 1)
        for tl in range(tk):
            for c, cp in enumerate(copies(nxt, 1 - slot, tl)):
                cp.start(priority=(2 * tl + c) % DMA_THREADS)
        ya, yb = bufs[slot]
        c1 = mf_ref[:, 0:1]
        c2 = mf_ref[:, 1:2]
        moe = jnp.concatenate(
            [c1 * sa + c2 * sb for sa, sb in zip(_load_slabs(ya, tk), _load_slabs(yb, tk))], axis=1)
        xo = x1_ref[...] + mod_ref[0, 5:6, :] * moe
        ms = jnp.mean(xo * xo, axis=-1, keepdims=True)
        o_ref[...] = (xo * lax.rsqrt(ms + RMS_EPS)) * gf_ref[...]

        @pl.when(i == n - 1)
        def _():
            drain(nxt, 1 - slot)

    @pl.when(i % 2 == 0)
    def _():
        step(0)

    @pl.when(i % 2 == 1)
    def _():
        step(1)


def _combine(d1, d2, yr, x1, mf, mod3, g_final, seq):
    t, d = x1.shape
    slabs = yr.shape[1]
    per_b = seq // COMBINE_TK
    grid_spec = pltpu.PrefetchScalarGridSpec(
        num_scalar_prefetch=2,
        grid=(t // COMBINE_TK,),
        in_specs=[
            pl.BlockSpec(memory_space=pl.ANY),
            pl.BlockSpec((COMBINE_TK, d), lambda i, a, b: (i, 0)),
            pl.BlockSpec((COMBINE_TK, ROUTER_LANES), lambda i, a, b: (i, 0)),
            pl.BlockSpec((1, N_MOD, d), lambda i, a, b: (i // per_b, 0, 0)),
            pl.BlockSpec((1, d), lambda i, a, b: (0, 0)),
        ],
        out_specs=pl.BlockSpec((COMBINE_TK, d), lambda i, a, b: (i, 0)),
        scratch_shapes=[pltpu.VMEM((COMBINE_TK * slabs, LANES), U32) for _ in range(4)]
        + [pltpu.SemaphoreType.DMA((2,))],
    )
    return pl.pallas_call(
        _combine_kernel,
        out_shape=jax.ShapeDtypeStruct((t, d), F32),
        grid_spec=grid_spec,
        compiler_params=pltpu.CompilerParams(
            dimension_semantics=("arbitrary",), vmem_limit_bytes=VMEM_LIMIT),
        name="combine",
    )(d1, d2, yr, x1, mf, mod3, g_final)


def _block_schedule(counts, n_rows):
    nblk = n_rows // MOE_BLK
    eidx = jnp.arange(N_EXPERTS, dtype=I32)
    bidx = jnp.arange(nblk, dtype=I32)
    nblocks = (counts + MOE_BLK - 1) // MOE_BLK
    bends = jnp.cumsum(nblocks)
    bstarts = bends - nblocks
    n_used = bends[-1]
    used = bidx < n_used
    blk_row = jnp.minimum(bidx, jnp.maximum(n_used - 1, 0))
    owner = jnp.logical_and(blk_row[:, None] >= bstarts[None, :], blk_row[:, None] < bends[None, :])

    def pick(per_expert):
        return jnp.sum(jnp.where(owner, per_expert[None, :], 0), axis=1).astype(I32)

    blk_e = pick(eidx)
    blk_valid = jnp.where(
        used, jnp.clip(pick(counts) - (bidx - pick(bstarts)) * MOE_BLK, 0, MOE_BLK), 0).astype(I32)
    blk_first = jnp.logical_and(used, bidx == pick(bstarts)).astype(I32)
    nonempty = counts > 0
    ordinal = jnp.cumsum(nonempty.astype(I32)) - 1
    blk_slot = pick(ordinal) % WEIGHT_SLOTS

    def ahead(k):
        match = jnp.logical_and(nonempty[None, :], ordinal[None, :] == ordinal[:, None] + k)
        found = jnp.sum(jnp.where(match, eidx[None, :], 0), axis=1)
        return pick(jnp.where(ordinal + k <= ordinal[-1], found, -1))

    blk_ahead = tuple(ahead(k) for k in range(1, WEIGHT_SLOTS))
    pstarts = (bstarts * MOE_BLK).astype(I32)
    return pstarts, (blk_e, blk_row, blk_valid, blk_first, blk_slot) + blk_ahead


def kernel(x, c, w_ada, b_ada, g_norm1, w_in, w_pool, pool_scale, attn_sinks, w_out,
           g_norm2, w_router_group, b_router_group, w_router_expert, b_router_expert,
           w_gate, w_up, w_down, g_final):
    bsz, seq, d = x.shape
    t = bsz * seq
    assert w_ada.shape[0] == 1, "single-layer model: the combine step applies the final norm"
    assert seq % INPROJ_TM == 0 and seq % MIX_TQ == 0 and seq % COMBINE_TK == 0, seq
    assert t % min(DISPATCH_TC, t) == 0 and d % (2 * LANES) == 0, (t, d)
    x2 = x.reshape(t, d)
    for l in range(1):
        mod3 = _ada(c, w_ada[l], b_ada[l]).reshape(bsz, N_MOD, d)
        proj = _inproj(x2, mod3, g_norm1[l].reshape(1, d), w_in[l].astype(BF16), seq)
        pad = ROUTER_LANES - N_EXPERT_GROUPS - N_EXPERTS
        w_r = jnp.concatenate(
            [w_router_group[l], w_router_expert[l], jnp.zeros((d, pad), F32)], axis=1).astype(BF16)
        b_r = jnp.concatenate(
            [b_router_group[l], b_router_expert[l], jnp.zeros((pad,), F32)]).reshape(1, ROUTER_LANES)
        x1, h2r, mi, mf, cnt = _mix(
            attn_sinks[l], proj, x2, mod3, w_pool[l].astype(BF16),
            pool_scale[l].reshape(1, D_POOL), w_out[l].astype(BF16),
            g_norm2[l].reshape(1, d), w_r, b_r, bsz, seq)
        n_rows = 2 * t + N_EXPERTS * MOE_BLK
        counts = cnt[0, N_EXPERT_GROUPS:N_EXPERT_GROUPS + N_EXPERTS]
        pstarts, schedule = _block_schedule(counts, n_rows)
        dest = _dest(pstarts, mi)
        d1, d2 = dest[0], dest[1]
        xr = _dispatch(pstarts, counts, d1, d2, h2r.reshape(t, d // (2 * LANES), LANES), n_rows)
        yr = _experts(schedule, xr, w_gate[l], w_up[l], w_down[l])
        x2 = _combine(d1, d2, yr, x1, mf, mod3, g_final.reshape(1, d), seq)
    return x2.reshape(bsz, seq, d)
```
